```python
import math
import jax
import jax.numpy as jnp
from jax import lax
import numpy as np

D_MODEL = 2048
BATCH = 1
SEQ = 8192
DEPTH = 4

A_HEADS = 4
A_QK_DIM = 128
A_V_DIM = 2 * A_QK_DIM
A_QK_WIDTH = A_HEADS * 2 * A_QK_DIM
A_WIDTH = A_HEADS * A_V_DIM
B_WIDTH = D_MODEL - A_WIDTH
CONV_WIDTH = 3
EVEN_IN = 2 * A_QK_WIDTH + A_WIDTH + 3 * B_WIDTH
C_Q_HEADS = 32
C_KV_HEADS = 4
C_HEAD_DIM = 64
C_GROUP = C_Q_HEADS // C_KV_HEADS
C_Q_WIDTH = C_Q_HEADS * C_HEAD_DIM
C_KV_WIDTH = C_KV_HEADS * C_HEAD_DIM
ODD_IN = C_Q_WIDTH + 2 * C_KV_WIDTH
WINDOW = 128
BLOCK = 128
ROPE_THETA = 500000.0
A_ROT_DIM = A_QK_DIM // 4
C_ROT_DIM = C_HEAD_DIM // 4
N_EXPERTS = 64
TOP_K = 8
N_GROUPS = 8
TOPK_GROUPS = 4
EXPERTS_PER_GROUP = N_EXPERTS // N_GROUPS
EXPERT_FF = 512
SHARED_FF = 512
ROUTE_SCALE = 2.5
MOE_BLOCK = 128
DN_ALPHA = (2.0 * DEPTH) ** 0.25
DN_BETA = (8.0 * DEPTH) ** -0.25
LN_EPS = 1e-5
N_EVEN = (DEPTH + 1) // 2
N_ODD = DEPTH // 2

kernel_name = "hybrid_diffattn_shortconv_swa_moe_encoder"

F32 = jnp.float32


def layer_norm(x, g, b):
    xf = x.astype(F32)
    mu = jnp.mean(xf, -1, keepdims=True)
    var = jnp.mean(jnp.square(xf - mu), -1, keepdims=True)
    return ((xf - mu) * lax.rsqrt(var + LN_EPS) * g.astype(F32) + b.astype(F32)).astype(x.dtype)


def rms_norm(x, g):
    xf = x.astype(F32)
    return (xf * lax.rsqrt(jnp.mean(xf * xf, -1, keepdims=True) + LN_EPS) * g.astype(F32)).astype(x.dtype)


def rope_tables(positions, rot_dim):
    half = rot_dim // 2
    inv_freq = ROPE_THETA ** (-jnp.arange(half, dtype=F32) * 2.0 / rot_dim)
    ang = positions.astype(F32)[..., None] * inv_freq
    return jnp.cos(ang)[:, :, None, :], jnp.sin(ang)[:, :, None, :]


def apply_partial_rope(x, cos, sin):
    r = cos.shape[-1]
    x1 = x[..., :r].astype(F32)
    x2 = x[..., r:2 * r].astype(F32)
    rot = jnp.concatenate([x1 * cos - x2 * sin, x2 * cos + x1 * sin], -1).astype(x.dtype)
    return jnp.concatenate([rot, x[..., 2 * r:]], -1)


def diff_attention(q, k, v, lam, subln_g, lam_init):
    bn, s_len, h, _, dk = q.shape
    dv = v.shape[-1]
    nb = s_len // BLOCK
    scale = dk ** -0.5
    qb = jnp.moveaxis(q.reshape(bn, nb, BLOCK, h, 2, dk), 1, 0)

    def one_block(qblk):
        s = jnp.einsum('bqhmd,bkhmd->bhmqk', qblk, k, preferred_element_type=F32) * scale
        p = jax.nn.softmax(s, axis=-1)
        a = p[:, :, 0] - lam * p[:, :, 1]
        return jnp.einsum('bhqk,bkhe->bqhe', a.astype(v.dtype), v)

    o = lax.map(one_block, qb)
    o = jnp.moveaxis(o, 0, 1).reshape(bn, s_len, h, dv)
    o = rms_norm(o, subln_g) * (1.0 - lam_init)
    return o.reshape(bn, s_len, h * dv)


def short_conv(h, b_gate, c_gate, conv_w):
    u = c_gate * h
    width = u.shape[-1]
    y = lax.conv_general_dilated(
        u, conv_w[:, None, :].astype(u.dtype), window_strides=(1,),
        padding=((CONV_WIDTH // 2, CONV_WIDTH // 2),),
        dimension_numbers=('NWC', 'WIO', 'NWC'), feature_group_count=width)
    return b_gate * y


def window_gqa(q, k, v, sinks):
    bn, s_len, hkv, g, hd = q.shape
    nb = s_len // BLOCK
    pad = ((0, 0), (BLOCK, BLOCK), (0, 0), (0, 0))
    kp = jnp.pad(k, pad)
    vp = jnp.pad(v, pad)
    qb = jnp.moveaxis(q.reshape(bn, nb, BLOCK, hkv, g, hd), 1, 0)
    rel = jnp.arange(3 * BLOCK)[None, :] - BLOCK - jnp.arange(BLOCK)[:, None]
    in_window = jnp.abs(rel) <= WINDOW
    sink = sinks.astype(F32).reshape(1, hkv, g, 1, 1)
    scale = hd ** -0.5

    def one_block(args):
        qblk, n = args
        start = n * BLOCK
        kblk = lax.dynamic_slice_in_dim(kp, start, 3 * BLOCK, axis=1)
        vblk = lax.dynamic_slice_in_dim(vp, start, 3 * BLOCK, axis=1)
        kpos = start - BLOCK + jnp.arange(3 * BLOCK)
        valid = in_window & ((kpos >= 0) & (kpos < s_len))[None, :]
        s = jnp.einsum('bqhgd,bkhd->bhgqk', qblk, kblk, preferred_element_type=F32) * scale
        s = jnp.where(valid, s, -jnp.inf)
        m = jnp.maximum(jnp.max(s, -1, keepdims=True), sink)
        e = jnp.exp(s - m)
        p = e / (jnp.sum(e, -1, keepdims=True) + jnp.exp(sink - m))
        return jnp.einsum('bhgqk,bkhd->bqhgd', p.astype(v.dtype), vblk)

    o = lax.map(one_block, (qb, jnp.arange(nb, dtype=jnp.int32)))
    return jnp.moveaxis(o, 0, 1).reshape(bn, s_len, hkv * g * hd)


def swiglu(x, wg, wu, wd):
    return (jax.nn.silu(x @ wg) * (x @ wu)) @ wd


def route(x2d, w_router, b_router):
    t = x2d.shape[0]
    scores = jax.nn.sigmoid(jnp.dot(x2d, w_router, preferred_element_type=F32))
    choice = scores + b_router.astype(F32)
    grp_score = jnp.sum(lax.top_k(choice.reshape(t, N_GROUPS, EXPERTS_PER_GROUP), 2)[0], -1)
    _, gidx = lax.top_k(grp_score, TOPK_GROUPS)
    gmask = jnp.sum(jax.nn.one_hot(gidx, N_GROUPS, dtype=F32), 1) > 0
    masked = jnp.where(jnp.repeat(gmask, EXPERTS_PER_GROUP, axis=1), choice, -jnp.inf)
    _, eidx = lax.top_k(masked, TOP_K)
    w = jnp.take_along_axis(scores, eidx, axis=1)
    w = w / jnp.sum(w, -1, keepdims=True) * ROUTE_SCALE
    return eidx, w


def routed_experts(x2d, eidx, w, w_gate, w_up, w_down):
    t, d = x2d.shape
    n_assign = t * TOP_K
    flat_e = eidx.reshape(n_assign).astype(jnp.int32)
    flat_tok = jnp.arange(n_assign, dtype=jnp.int32) // TOP_K
    flat_w = w.reshape(n_assign)
    order = jnp.argsort(flat_e)
    e_sorted = flat_e[order]
    counts = jnp.bincount(flat_e, length=N_EXPERTS)
    padded = (counts + MOE_BLOCK - 1) // MOE_BLOCK * MOE_BLOCK
    pad_end = jnp.cumsum(padded)
    pad_start = pad_end - padded
    start = jnp.cumsum(counts) - counts
    dest = pad_start[e_sorted] + jnp.arange(n_assign, dtype=jnp.int32) - start[e_sorted]
    n_blocks = -(-n_assign // MOE_BLOCK) + N_EXPERTS
    rows = n_blocks * MOE_BLOCK
    tok_pad = jnp.full((rows,), t, jnp.int32).at[dest].set(flat_tok[order])
    w_pad = jnp.zeros((rows,), F32).at[dest].set(flat_w[order])
    blk_e = jnp.minimum(jnp.searchsorted(pad_end, jnp.arange(n_blocks, dtype=jnp.int32) * MOE_BLOCK,
                                         side='right'), N_EXPERTS - 1).astype(jnp.int32)
    x_ext = jnp.concatenate([x2d, jnp.zeros((1, d), x2d.dtype)], 0)

    def step(acc, inp):
        e, tok, wt = inp
        xb = x_ext[tok]
        yb = swiglu(xb, w_gate[e], w_up[e], w_down[e]) * wt[:, None].astype(x2d.dtype)
        return acc.at[tok].add(yb), None

    acc, _ = lax.scan(step, jnp.zeros((t + 1, d), x2d.dtype),
                      (blk_e, tok_pad.reshape(n_blocks, MOE_BLOCK), w_pad.reshape(n_blocks, MOE_BLOCK)))
    return acc[:t]


def setup_inputs(seed: int = 0) -> dict:
    key = jax.random.key(seed)
    ks = jax.random.split(key, 24)

    def nrm(k, shape, scale):
        return jax.random.normal(k, shape, F32) * scale

    x = nrm(ks[0], (BATCH, SEQ, D_MODEL), 1.0)
    positions = jnp.broadcast_to(jnp.arange(SEQ, dtype=jnp.int32)[None, :], (BATCH, SEQ))
    even_col = jnp.concatenate([jnp.ones((2 * A_QK_WIDTH,), F32), jnp.full((A_WIDTH,), DN_BETA, F32),
                                jnp.ones((3 * B_WIDTH,), F32)])
    even_w_in = nrm(ks[1], (N_EVEN, D_MODEL, EVEN_IN), D_MODEL ** -0.5) * even_col
    even_w_out = nrm(ks[2], (N_EVEN, A_WIDTH + B_WIDTH, D_MODEL), (A_WIDTH + B_WIDTH) ** -0.5 * DN_BETA)
    a_lambda = nrm(ks[3], (N_EVEN, 4, A_QK_DIM), 0.1)
    a_subln_g = 1.0 + nrm(ks[4], (N_EVEN, A_V_DIM), 0.02)
    b_conv_w = nrm(ks[5], (N_EVEN, CONV_WIDTH, B_WIDTH), CONV_WIDTH ** -0.5)
    odd_col = jnp.concatenate([jnp.ones((C_Q_WIDTH + C_KV_WIDTH,), F32), jnp.full((C_KV_WIDTH,), DN_BETA, F32)])
    odd_w_in = nrm(ks[6], (N_ODD, D_MODEL, ODD_IN), D_MODEL ** -0.5) * odd_col
    odd_w_out = nrm(ks[7], (N_ODD, C_Q_WIDTH, D_MODEL), C_Q_WIDTH ** -0.5 * DN_BETA)
    c_sinks = nrm(ks[8], (N_ODD, C_Q_HEADS), 0.5)
    ln1_g = 1.0 + nrm(ks[9], (DEPTH, D_MODEL), 0.02)
    ln1_b = nrm(ks[10], (DEPTH, D_MODEL), 0.02)
    ln2_g = 1.0 + nrm(ks[11], (DEPTH, D_MODEL), 0.02)
    ln2_b = nrm(ks[12], (DEPTH, D_MODEL), 0.02)
    router_w = nrm(ks[13], (DEPTH, D_MODEL, N_EXPERTS), D_MODEL ** -0.5)
    router_b = nrm(ks[14], (DEPTH, N_EXPERTS), 0.01)
    exp_w_gate = nrm(ks[15], (DEPTH, N_EXPERTS, D_MODEL, EXPERT_FF), D_MODEL ** -0.5)
    exp_w_up = nrm(ks[16], (DEPTH, N_EXPERTS, D_MODEL, EXPERT_FF), D_MODEL ** -0.5)
    exp_w_down = nrm(ks[17], (DEPTH, N_EXPERTS, EXPERT_FF, D_MODEL), EXPERT_FF ** -0.5 * DN_BETA)
    sh_w_gate = nrm(ks[18], (DEPTH, D_MODEL, SHARED_FF), D_MODEL ** -0.5)
    sh_w_up = nrm(ks[19], (DEPTH, D_MODEL, SHARED_FF), D_MODEL ** -0.5)
    sh_w_down = nrm(ks[20], (DEPTH, SHARED_FF, D_MODEL), SHARED_FF ** -0.5 * DN_BETA)
    return {"x": x, "positions": positions, "even_w_in": even_w_in, "even_w_out": even_w_out,
            "a_lambda": a_lambda, "a_subln_g": a_subln_g, "b_conv_w": b_conv_w,
            "odd_w_in": odd_w_in, "odd_w_out": odd_w_out, "c_sinks": c_sinks,
            "ln1_g": ln1_g, "ln1_b": ln1_b, "ln2_g": ln2_g, "ln2_b": ln2_b,
            "router_w": router_w, "router_b": router_b,
            "exp_w_gate": exp_w_gate, "exp_w_up": exp_w_up, "exp_w_down": exp_w_down,
            "sh_w_gate": sh_w_gate, "sh_w_up": sh_w_up, "sh_w_down": sh_w_down}


def reference(x, positions, even_w_in, even_w_out, a_lambda, a_subln_g, b_conv_w,
              odd_w_in, odd_w_out, c_sinks, ln1_g, ln1_b, ln2_g, ln2_b,
              router_w, router_b, exp_w_gate, exp_w_up, exp_w_down,
              sh_w_gate, sh_w_up, sh_w_down):
    bn, s_len, d = x.shape
    cos_a, sin_a = rope_tables(positions, A_ROT_DIM)
    cos_c, sin_c = rope_tables(positions, C_ROT_DIM)
    split_even = [A_QK_WIDTH, 2 * A_QK_WIDTH, 2 * A_QK_WIDTH + A_WIDTH,
                  2 * A_QK_WIDTH + A_WIDTH + B_WIDTH, 2 * A_QK_WIDTH + A_WIDTH + 2 * B_WIDTH]
    for layer in range(DEPTH):
        j = layer // 2
        if layer % 2 == 0:
            h = x @ even_w_in[j]
            q, k, v, b_gate, c_gate, hb = jnp.split(h, split_even, axis=-1)
            q = apply_partial_rope(q.reshape(bn, s_len, 2 * A_HEADS, A_QK_DIM), cos_a, sin_a)
            k = apply_partial_rope(k.reshape(bn, s_len, 2 * A_HEADS, A_QK_DIM), cos_a, sin_a)
            q = q.reshape(bn, s_len, A_HEADS, 2, A_QK_DIM)
            k = k.reshape(bn, s_len, A_HEADS, 2, A_QK_DIM)
            v = v.reshape(bn, s_len, A_HEADS, A_V_DIM)
            lam_init = 0.8 - 0.6 * math.exp(-0.3 * layer)
            lv = a_lambda[j].astype(F32)
            lam = jnp.exp(jnp.sum(lv[0] * lv[1])) - jnp.exp(jnp.sum(lv[2] * lv[3])) + lam_init
            y_a = diff_attention(q, k, v, lam, a_subln_g[j], lam_init)
            y_b = short_conv(hb, b_gate, c_gate, b_conv_w[j])
            y = jnp.concatenate([y_a, y_b], axis=-1) @ even_w_out[j]
        else:
            h = x @ odd_w_in[j]
            q, k, v = jnp.split(h, [C_Q_WIDTH, C_Q_WIDTH + C_KV_WIDTH], axis=-1)
            q = apply_partial_rope(q.reshape(bn, s_len, C_Q_HEADS, C_HEAD_DIM), cos_c, sin_c)
            q = q.reshape(bn, s_len, C_KV_HEADS, C_GROUP, C_HEAD_DIM)
            k = apply_partial_rope(k.reshape(bn, s_len, C_KV_HEADS, C_HEAD_DIM), cos_c, sin_c)
            v = v.reshape(bn, s_len, C_KV_HEADS, C_HEAD_DIM)
            y = window_gqa(q, k, v, c_sinks[j]) @ odd_w_out[j]
        x = layer_norm(DN_ALPHA * x + y, ln1_g[layer], ln1_b[layer])
        x2d = x.reshape(bn * s_len, d)
        eidx, gate_w = route(x2d, router_w[layer], router_b[layer])
        y = routed_experts(x2d, eidx, gate_w, exp_w_gate[layer], exp_w_up[layer], exp_w_down[layer])
        y = y + swiglu(x2d, sh_w_gate[layer], sh_w_up[layer], sh_w_down[layer])
        x = layer_norm(DN_ALPHA * x + y.reshape(bn, s_len, d), ln2_g[layer], ln2_b[layer])
    return x
```

```python
import functools
import math

import jax
import jax.numpy as jnp
from jax import lax
from jax.experimental import pallas as pl
from jax.experimental.pallas import tpu as pltpu

F32 = jnp.float32
BF16 = jnp.bfloat16

LANES = 128
VMEM_LIMIT_BYTES = 56 * 1024 * 1024

A_HEADS = 4
A_QK_DIM = 128
A_V_DIM = 256
A_QK_WIDTH = 1024
A_WIDTH = 1024
B_WIDTH = 1024
C_Q_HEADS = 32
C_KV_HEADS = 4
C_HEAD_DIM = 64
C_GROUP = C_Q_HEADS // C_KV_HEADS
C_Q_WIDTH = C_Q_HEADS * C_HEAD_DIM
C_KV_WIDTH = C_KV_HEADS * C_HEAD_DIM
WINDOW = 128
ROPE_THETA = 500000.0
A_ROT_DIM = A_QK_DIM // 4
C_ROT_DIM = C_HEAD_DIM // 4
N_EXPERTS = 64
TOP_K = 8
N_GROUPS = 8
TOPK_GROUPS = 4
EXPERTS_PER_GROUP = N_EXPERTS // N_GROUPS
ROUTE_SCALE = 2.5
LN_EPS = 1e-5
LOG2E = math.log2(math.e)

GMM_BLOCK = 256


def _cparams(semantics):
    return pltpu.CompilerParams(dimension_semantics=semantics, vmem_limit_bytes=VMEM_LIMIT_BYTES)


def _tile(dim, want):
    t = min(dim, want)
    assert dim % t == 0, (dim, want)
    return t


def _rope_tables(positions, head_dim, rot_dim):
    half = rot_dim // 2
    inv_freq = ROPE_THETA ** (-jnp.arange(half, dtype=F32) * 2.0 / rot_dim)
    ang = positions.astype(F32)[:, None] * inv_freq
    cos, sin = jnp.cos(ang), jnp.sin(ang)
    lane = jnp.arange(LANES) % head_dim
    idx = lane % half
    cos_l = jnp.take(cos, idx, axis=1)
    sin_l = jnp.take(sin, idx, axis=1)
    c = jnp.where(lane < 2 * half, cos_l, 1.0)
    s1 = jnp.where(lane < half, -sin_l, 0.0)
    s2 = jnp.where((lane >= half) & (lane < 2 * half), sin_l, 0.0)
    return c.astype(F32), s1.astype(F32), s2.astype(F32)


def _proj_in_kernel(x_ref, w_ref, c_ref, s1_ref, s2_ref, o_ref, *, tile_kinds, half, q_scale):
    j = pl.program_id(0)
    acc = jnp.dot(x_ref[...].astype(BF16), w_ref[...], preferred_element_type=F32)
    n_chunks = acc.shape[1] // LANES
    groups = {}
    for t, kinds in enumerate(tile_kinds):
        groups.setdefault(kinds, []).append(t)
    for kinds, tiles in groups.items():
        cond = functools.reduce(jnp.logical_or, [j == t for t in tiles])

        @pl.when(cond)
        def _(kinds=kinds):
            for c in range(n_chunks):
                r = acc[:, c * LANES:(c + 1) * LANES]
                if kinds[c] != "n":
                    r = (r * c_ref[...] + pltpu.roll(r, LANES - half, 1) * s1_ref[...]
                         + pltpu.roll(r, half, 1) * s2_ref[...])
                    if kinds[c] == "q":
                        r = r * q_scale
                o_ref[:, c * LANES:(c + 1) * LANES] = r.astype(o_ref.dtype)


def _proj_in(x, w, tables, *, q_width, k_width, half, q_scale, tm, tn):
    m, kdim = x.shape
    n = w.shape[1]
    tm = _tile(m, tm)
    tn = _tile(n, tn)

    def kind(col):
        return "q" if col < q_width else ("k" if col < q_width + k_width else "n")

    tile_kinds = tuple(tuple(kind(t * tn + c * LANES) for c in range(tn // LANES)) for t in range(n // tn))
    c, s1, s2 = tables
    tab_spec = pl.BlockSpec((tm, LANES), lambda j, i: (i, 0))
    return pl.pallas_call(
        functools.partial(_proj_in_kernel, tile_kinds=tile_kinds, half=half, q_scale=q_scale),
        grid=(n // tn, m // tm),
        in_specs=[pl.BlockSpec((tm, kdim), lambda j, i: (i, 0)),
                  pl.BlockSpec((kdim, tn), lambda j, i: (0, j)),
                  tab_spec, tab_spec, tab_spec],
        out_specs=pl.BlockSpec((tm, tn), lambda j, i: (i, j)),
        out_shape=jax.ShapeDtypeStruct((m, n), BF16),
        compiler_params=_cparams(("arbitrary", "arbitrary")),
        name="proj_in",
    )(x, w, c, s1, s2)


def _diff_attn_kernel(lam_ref, q_ref, k_ref, v_ref, g_ref, o_ref, m_ref, l_ref, acc_ref, *, tk, lam_init):
    s_len = k_ref.shape[0]
    dk = A_QK_DIM
    nt = (((1,), (1,)), ((), ()))
    m_ref[...] = jnp.full(m_ref.shape, -jnp.inf, F32)
    l_ref[...] = jnp.zeros(l_ref.shape, F32)
    acc_ref[...] = jnp.zeros(acc_ref.shape, F32)

    def body(kk, carry):
        ks = pl.multiple_of(kk * tk, tk)
        vb = v_ref[pl.ds(ks, tk), :]
        for mi in range(2):
            qm = q_ref[:, mi * dk:(mi + 1) * dk]
            kb = k_ref[pl.ds(ks, tk), mi * dk:(mi + 1) * dk]
            s = lax.dot_general(qm, kb, nt, preferred_element_type=F32)
            m_old = m_ref[mi]
            m_new = jnp.maximum(m_old, jnp.max(s, axis=-1, keepdims=True))
            alpha = jnp.exp2(m_old - m_new)
            p = jnp.exp2(s - m_new)
            l_ref[mi] = alpha * l_ref[mi] + jnp.sum(p, axis=-1, keepdims=True)
            acc_ref[mi] = alpha * acc_ref[mi] + jnp.dot(p.astype(BF16), vb, preferred_element_type=F32)
            m_ref[mi] = m_new
        return carry

    lax.fori_loop(0, s_len // tk, body, 0)
    lam = lam_ref[0]
    o = acc_ref[0] / l_ref[0] - lam * (acc_ref[1] / l_ref[1])
    ms = jnp.mean(o * o, axis=-1, keepdims=True)
    o = o * lax.rsqrt(ms + LN_EPS) * g_ref[...] * (1.0 - lam_init)
    o_ref[...] = o.astype(o_ref.dtype)


def _diff_attn(h, lam, subln_g, *, lam_init, tq, tk):
    s_len = h.shape[0]
    tq = _tile(s_len, tq)
    tk = _tile(s_len, tk)
    w = 2 * A_QK_DIM
    k_off = A_QK_WIDTH // w
    v_off = 2 * A_QK_WIDTH // w
    return pl.pallas_call(
        functools.partial(_diff_attn_kernel, tk=tk, lam_init=lam_init),
        grid=(A_HEADS, s_len // tq),
        in_specs=[pl.BlockSpec(memory_space=pltpu.SMEM),
                  pl.BlockSpec((tq, w), lambda hh, i: (i, hh)),
                  pl.BlockSpec((s_len, w), lambda hh, i: (0, k_off + hh)),
                  pl.BlockSpec((s_len, A_V_DIM), lambda hh, i: (0, v_off + hh)),
                  pl.BlockSpec((1, A_V_DIM), lambda hh, i: (0, 0))],
        out_specs=pl.BlockSpec((tq, A_V_DIM), lambda hh, i: (i, hh)),
        out_shape=jax.ShapeDtypeStruct((s_len, A_WIDTH), BF16),
        scratch_shapes=[pltpu.VMEM((2, tq, 1), F32), pltpu.VMEM((2, tq, 1), F32),
                        pltpu.VMEM((2, tq, A_V_DIM), F32)],
        compiler_params=_cparams(("arbitrary", "arbitrary")),
        name="diff_attn",
    )(lam, h, h, h, subln_g.reshape(1, A_V_DIM).astype(F32))


HALO = 16


def _short_conv_kernel(bg_ref, cg_ref, hb_ref, cgp_ref, hbp_ref, cgn_ref, hbn_ref, w_ref, o_ref):
    i = pl.program_id(0)
    n_i = pl.num_programs(0)
    u = cg_ref[...].astype(F32) * hb_ref[...].astype(F32)
    tr = u.shape[0]
    prev_row = cgp_ref[HALO - 1:HALO, :].astype(F32) * hbp_ref[HALO - 1:HALO, :].astype(F32)
    next_row = cgn_ref[0:1, :].astype(F32) * hbn_ref[0:1, :].astype(F32)
    prev_row = jnp.where(i == 0, 0.0, prev_row)
    next_row = jnp.where(i == n_i - 1, 0.0, next_row)
    row = lax.broadcasted_iota(jnp.int32, u.shape, 0)
    u_prev = jnp.where(row == 0, prev_row, pltpu.roll(u, 1, 0))
    u_next = jnp.where(row == tr - 1, next_row, pltpu.roll(u, tr - 1, 0))
    w = w_ref[...]
    y = w[0:1, :] * u_prev + w[1:2, :] * u + w[2:3, :] * u_next
    o_ref[...] = (bg_ref[...].astype(F32) * y).astype(o_ref.dtype)


def _short_conv(h, conv_w, *, tr, tc):
    s_len = h.shape[0]
    tr = _tile(s_len, tr)
    tc = _tile(B_WIDTH, tc)
    base = 2 * A_QK_WIDTH + A_WIDTH
    bg_off, cg_off, hb_off = base // tc, (base + B_WIDTH) // tc, (base + 2 * B_WIDTH) // tc
    rb = tr // HALO
    last = s_len // HALO - 1

    def main(off):
        return pl.BlockSpec((tr, tc), lambda i, c: (i, off + c))

    def prev(off):
        return pl.BlockSpec((HALO, tc), lambda i, c: (jnp.maximum(i * rb - 1, 0), off + c))

    def nxt(off):
        return pl.BlockSpec((HALO, tc), lambda i, c: (jnp.minimum((i + 1) * rb, last), off + c))

    return pl.pallas_call(
        _short_conv_kernel,
        grid=(s_len // tr, B_WIDTH // tc),
        in_specs=[main(bg_off), main(cg_off), main(hb_off), prev(cg_off), prev(hb_off), nxt(cg_off), nxt(hb_off),
                  pl.BlockSpec((3, tc), lambda i, c: (0, c))],
        out_specs=pl.BlockSpec((tr, tc), lambda i, c: (i, c)),
        out_shape=jax.ShapeDtypeStruct((s_len, B_WIDTH), BF16),
        compiler_params=_cparams(("arbitrary", "arbitrary")),
        name="short_conv",
    )(h, h, h, h, h, h, h, conv_w.astype(F32))


def _win_attn_kernel(sink_ref, q_ref, k_ref, v_ref, o_ref, *, tq):
    i = pl.program_id(0)
    s_len = k_ref.shape[0]
    win = tq + 2 * WINDOW
    hd = C_HEAD_DIM
    nt = (((1,), (1,)), ((), ()))
    q_start = i * tq
    k_start = pl.multiple_of(jnp.clip(q_start - WINDOW, 0, s_len - win), WINDOW)
    qpos = q_start + lax.broadcasted_iota(jnp.int32, (tq, win), 0)
    kpos = k_start + lax.broadcasted_iota(jnp.int32, (tq, win), 1)
    valid = jnp.abs(qpos - kpos) <= WINDOW
    kw = k_ref[pl.ds(k_start, win), :]
    vw = v_ref[pl.ds(k_start, win), :]
    for g in range(C_KV_HEADS):
        kg = kw[:, g * hd:(g + 1) * hd]
        vg = vw[:, g * hd:(g + 1) * hd]
        for j in range(C_GROUP):
            qh = g * C_GROUP + j
            q = q_ref[:, qh * hd:(qh + 1) * hd]
            s = lax.dot_general(q, kg, nt, preferred_element_type=F32)
            s = jnp.where(valid, s, -jnp.inf)
            sink = sink_ref[qh]
            m = jnp.maximum(jnp.max(s, axis=-1, keepdims=True), sink)
            e = jnp.exp2(s - m)
            denom = jnp.sum(e, axis=-1, keepdims=True) + jnp.exp2(sink - m)
            p = e / denom
            o = jnp.dot(p.astype(BF16), vg, preferred_element_type=F32)
            o_ref[:, qh * hd:(qh + 1) * hd] = o.astype(o_ref.dtype)


def _win_attn(h, sinks_log2, *, tq):
    s_len = h.shape[0]
    tq = _tile(s_len, tq)
    assert s_len >= tq + 2 * WINDOW
    k_off = C_Q_WIDTH // C_KV_WIDTH
    return pl.pallas_call(
        functools.partial(_win_attn_kernel, tq=tq),
        grid=(s_len // tq,),
        in_specs=[pl.BlockSpec(memory_space=pltpu.SMEM),
                  pl.BlockSpec((tq, C_Q_WIDTH), lambda i: (i, 0)),
                  pl.BlockSpec((s_len, C_KV_WIDTH), lambda i: (0, k_off)),
                  pl.BlockSpec((s_len, C_KV_WIDTH), lambda i: (0, k_off + 1))],
        out_specs=pl.BlockSpec((tq, C_Q_WIDTH), lambda i: (i, 0)),
        out_shape=jax.ShapeDtypeStruct((s_len, C_Q_WIDTH), BF16),
        compiler_params=_cparams(("arbitrary",)),
        name="win_attn",
    )(sinks_log2, h, h, h)


def _layer_norm(z, g, b):
    mu = jnp.mean(z, axis=-1, keepdims=True)
    zc = z - mu
    var = jnp.mean(zc * zc, axis=-1, keepdims=True)
    return zc * lax.rsqrt(var + LN_EPS) * g + b


def _proj_out_ln_kernel(*refs, n_in, alpha):
    y_refs = refs[:n_in]
    w_ref, x_ref, g_ref, b_ref, xo_ref, xb_ref = refs[n_in:]
    acc = None
    row = 0
    for y_ref in y_refs:
        kw = y_ref.shape[1]
        part = jnp.dot(y_ref[...], w_ref[row:row + kw, :], preferred_element_type=F32)
        acc = part if acc is None else acc + part
        row += kw
    z = _layer_norm(alpha * x_ref[...] + acc, g_ref[...], b_ref[...])
    xo_ref[...] = z
    xb_ref[...] = z.astype(BF16)


def _proj_out_ln(ys, w, x, g, b, *, alpha, tm):
    m, d = x.shape
    tm = _tile(m, tm)
    kdim = w.shape[0]
    in_specs = [pl.BlockSpec((tm, y.shape[1]), lambda i: (i, 0)) for y in ys]
    in_specs += [pl.BlockSpec((kdim, d), lambda i: (0, 0)),
                 pl.BlockSpec((tm, d), lambda i: (i, 0)),
                 pl.BlockSpec((1, d), lambda i: (0, 0)),
                 pl.BlockSpec((1, d), lambda i: (0, 0))]
    return pl.pallas_call(
        functools.partial(_proj_out_ln_kernel, n_in=len(ys), alpha=alpha),
        grid=(m // tm,),
        in_specs=in_specs,
        out_specs=[pl.BlockSpec((tm, d), lambda i: (i, 0)), pl.BlockSpec((tm, d), lambda i: (i, 0))],
        out_shape=[jax.ShapeDtypeStruct((m, d), F32), jax.ShapeDtypeStruct((m, d), BF16)],
        compiler_params=_cparams(("arbitrary",)),
        name="proj_out_ln",
    )(*ys, w, x, g.reshape(1, d).astype(F32), b.reshape(1, d).astype(F32))


def _first_argmax(vals, iota, size):
    mx = jnp.max(vals, axis=0, keepdims=True)
    idx = jnp.min(jnp.where(vals == mx, iota, size), axis=0, keepdims=True)
    return mx, idx


def _router_kernel(x_ref, wt_ref, b_ref, eidx_ref, gate_ref, rank_ref, cnt_ref, carry_ref):
    i = pl.program_id(0)
    tm = x_ref.shape[0]
    neg = -jnp.inf

    @pl.when(i == 0)
    def _():
        carry_ref[...] = jnp.zeros(carry_ref.shape, F32)

    logits = lax.dot_general(wt_ref[...], x_ref[...], (((1,), (1,)), ((), ())),
                             precision=lax.Precision.HIGHEST, preferred_element_type=F32)
    scores = jax.nn.sigmoid(logits)
    choice = scores + b_ref[...]
    iota_m = lax.broadcasted_iota(jnp.int32, (EXPERTS_PER_GROUP, tm), 0)
    gs_rows = []
    for g in range(N_GROUPS):
        cg = choice[g * EXPERTS_PER_GROUP:(g + 1) * EXPERTS_PER_GROUP, :]
        m1, i1 = _first_argmax(cg, iota_m, EXPERTS_PER_GROUP)
        m2 = jnp.max(jnp.where(iota_m == i1, neg, cg), axis=0, keepdims=True)
        gs_rows.append(m1 + m2)
    gs = jnp.concatenate(gs_rows, axis=0)
    iota_g = lax.broadcasted_iota(jnp.int32, (N_GROUPS, tm), 0)
    sel = jnp.zeros((N_GROUPS, tm), F32)
    for _ in range(TOPK_GROUPS):
        _, gi = _first_argmax(gs, iota_g, N_GROUPS)
        hit = iota_g == gi
        sel = jnp.where(hit, 1.0, sel)
        gs = jnp.where(hit, neg, gs)
    allowed = jnp.concatenate(
        [jnp.broadcast_to(sel[g:g + 1, :], (EXPERTS_PER_GROUP, tm)) for g in range(N_GROUPS)], axis=0)
    iota_e = lax.broadcasted_iota(jnp.int32, (N_EXPERTS, tm), 0)
    masked = jnp.where(allowed > 0.0, choice, neg)
    onehot = jnp.zeros((N_EXPERTS, tm), F32)
    e_rows, w_rows = [], []
    for _ in range(TOP_K):
        _, ei = _first_argmax(masked, iota_e, N_EXPERTS)
        hit = iota_e == ei
        e_rows.append(ei)
        w_rows.append(jnp.sum(jnp.where(hit, scores, 0.0), axis=0, keepdims=True))
        onehot = jnp.where(hit, 1.0, onehot)
        masked = jnp.where(hit, neg, masked)
    wsel = jnp.concatenate(w_rows, axis=0)
    gate_ref[...] = wsel / jnp.sum(wsel, axis=0, keepdims=True) * ROUTE_SCALE
    eidx_ref[...] = jnp.concatenate(e_rows, axis=0)
    tri = (lax.broadcasted_iota(jnp.int32, (tm, tm), 0) < lax.broadcasted_iota(jnp.int32, (tm, tm), 1))
    prefix = jnp.dot(onehot.astype(BF16), tri.astype(BF16), preferred_element_type=F32)
    base = prefix + carry_ref[...]
    r_rows = [jnp.sum(jnp.where(iota_e == ei, base, 0.0), axis=0, keepdims=True) for ei in e_rows]
    rank_ref[...] = jnp.concatenate(r_rows, axis=0).astype(jnp.int32)
    carry_ref[...] = carry_ref[...] + jnp.sum(onehot, axis=1, keepdims=True)
    cnt_ref[...] = carry_ref[...].astype(jnp.int32)


def _router(x, router_w, router_b, *, tm):
    t, d = x.shape
    tm = _tile(t, tm)
    wt = router_w.astype(F32).T
    kt_spec = pl.BlockSpec((TOP_K, tm), lambda i: (0, i))
    return pl.pallas_call(
        _router_kernel,
        grid=(t // tm,),
        in_specs=[pl.BlockSpec((tm, d), lambda i: (i, 0)),
                  pl.BlockSpec((N_EXPERTS, d), lambda i: (0, 0)),
                  pl.BlockSpec((N_EXPERTS, 1), lambda i: (0, 0))],
        out_specs=[kt_spec, kt_spec, kt_spec, pl.BlockSpec((N_EXPERTS, 1), lambda i: (0, 0))],
        out_shape=[jax.ShapeDtypeStruct((TOP_K, t), jnp.int32), jax.ShapeDtypeStruct((TOP_K, t), F32),
                   jax.ShapeDtypeStruct((TOP_K, t), jnp.int32), jax.ShapeDtypeStruct((N_EXPERTS, 1), jnp.int32)],
        scratch_shapes=[pltpu.VMEM((N_EXPERTS, 1), F32)],
        compiler_params=_cparams(("arbitrary",)),
        name="router",
    )(x, wt, router_b.astype(F32).reshape(N_EXPERTS, 1))


def _dispatch_kernel(pos_ref, x_ref, xs_hbm, sem):
    tm = x_ref.shape[0]

    def issue(t, carry):
        for k in range(TOP_K):
            p = pos_ref[k, t]
            pltpu.make_async_copy(x_ref.at[pl.ds(t, 1), :], xs_hbm.at[pl.ds(p, 1), :], sem).start()
        return carry

    lax.fori_loop(0, tm, issue, 0)
    for k in range(TOP_K):
        pltpu.make_async_copy(x_ref, xs_hbm.at[pl.ds(0, tm), :], sem).wait()


def _dispatch(x, pos, rows, *, tm):
    t, d = x.shape
    tm = _tile(t, tm)
    return pl.pallas_call(
        _dispatch_kernel,
        grid=(t // tm,),
        in_specs=[pl.BlockSpec((TOP_K, tm), lambda i: (0, i), memory_space=pltpu.SMEM),
                  pl.BlockSpec((tm, d), lambda i: (i, 0))],
        out_specs=pl.BlockSpec(memory_space=pl.ANY),
        out_shape=jax.ShapeDtypeStruct((rows, d), x.dtype),
        scratch_shapes=[pltpu.SemaphoreType.DMA(())],
        compiler_params=_cparams(("arbitrary",)),
        name="dispatch",
    )(pos, x)


def _gmm_kernel(blk_e_ref, nused_ref, xs_ref, wg_ref, wu_ref, wd_ref, o_ref, wgu_s, wd_s):
    b = pl.program_id(0)
    ff = wg_ref.shape[2]
    used = b < nused_ref[0]
    new_e = jnp.logical_or(b == 0, blk_e_ref[b] != blk_e_ref[jnp.maximum(b - 1, 0)])

    @pl.when(jnp.logical_and(used, new_e))
    def _():
        wgu_s[:, :ff] = wg_ref[0].astype(BF16)
        wgu_s[:, ff:] = wu_ref[0].astype(BF16)
        wd_s[...] = wd_ref[0].astype(BF16)

    @pl.when(used)
    def _():
        x = xs_ref[...].astype(BF16)
        gu = jnp.dot(x, wgu_s[...], preferred_element_type=F32)
        gate = gu[:, :ff]
        hmid = gate * jax.nn.sigmoid(gate) * gu[:, ff:]
        o_ref[...] = jnp.dot(hmid.astype(BF16), wd_s[...], preferred_element_type=F32)

    @pl.when(jnp.logical_not(used))
    def _():
        o_ref[...] = jnp.zeros(o_ref.shape, o_ref.dtype)


def _gmm(xs, w_gate, w_up, w_down, blk_e, nused, *, bm):
    rows, d = xs.shape
    ff = w_gate.shape[2]
    n_blocks = rows // bm
    grid_spec = pltpu.PrefetchScalarGridSpec(
        num_scalar_prefetch=2,
        grid=(n_blocks,),
        in_specs=[pl.BlockSpec((bm, d), lambda b, be, nu: (b, 0)),
                  pl.BlockSpec((1, d, ff), lambda b, be, nu: (be[b], 0, 0)),
                  pl.BlockSpec((1, d, ff), lambda b, be, nu: (be[b], 0, 0)),
                  pl.BlockSpec((1, ff, d), lambda b, be, nu: (be[b], 0, 0))],
        out_specs=pl.BlockSpec((bm, d), lambda b, be, nu: (b, 0)),
        scratch_shapes=[pltpu.VMEM((d, 2 * ff), BF16), pltpu.VMEM((ff, d), BF16)],
    )
    return pl.pallas_call(
        _gmm_kernel,
        grid_spec=grid_spec,
        out_shape=jax.ShapeDtypeStruct((rows, d), F32),
        compiler_params=_cparams(("arbitrary",)),
        name="gmm",
    )(blk_e, nused, xs, w_gate, w_up, w_down)


def _combine_kernel(pos_ref, os_hbm, gate_ref, x_ref, xb_ref, wgu_ref, wd_ref, g_ref, b_ref,
                    xo_ref, xbo_ref, buf, sem, *, alpha):
    tm = x_ref.shape[0]
    ff = wd_ref.shape[0]

    def issue(t, carry):
        for k in range(TOP_K):
            p = pos_ref[k, t]
            pltpu.make_async_copy(os_hbm.at[pl.ds(p, 1), :], buf.at[k, pl.ds(t, 1), :], sem).start()
        return carry

    lax.fori_loop(0, tm, issue, 0)
    gu = jnp.dot(xb_ref[...], wgu_ref[...], preferred_element_type=F32)
    gate = gu[:, :ff]
    hmid = gate * jax.nn.sigmoid(gate) * gu[:, ff:]
    y = jnp.dot(hmid.astype(BF16), wd_ref[...], preferred_element_type=F32)
    for k in range(TOP_K):
        pltpu.make_async_copy(os_hbm.at[pl.ds(0, tm), :], buf.at[k], sem).wait()
    routed = buf[0] * gate_ref[:, 0:1]
    for k in range(1, TOP_K):
        routed = routed + buf[k] * gate_ref[:, k:k + 1]
    z = _layer_norm(alpha * x_ref[...] + (routed + y), g_ref[...], b_ref[...])
    xo_ref[...] = z
    xbo_ref[...] = z.astype(BF16)


def _combine(os_, pos, gate_t, x, xb, sh_wgu, sh_wd, g, b, *, alpha, tm):
    t, d = x.shape
    tm = _tile(t, tm)
    ff = sh_wd.shape[0]
    row_spec = pl.BlockSpec((tm, d), lambda i: (i, 0))
    vec_spec = pl.BlockSpec((1, d), lambda i: (0, 0))
    return pl.pallas_call(
        functools.partial(_combine_kernel, alpha=alpha),
        grid=(t // tm,),
        in_specs=[pl.BlockSpec((TOP_K, tm), lambda i: (0, i), memory_space=pltpu.SMEM),
                  pl.BlockSpec(memory_space=pl.ANY),
                  pl.BlockSpec((tm, TOP_K), lambda i: (i, 0)),
                  row_spec, row_spec,
                  pl.BlockSpec((d, 2 * ff), lambda i: (0, 0)),
                  pl.BlockSpec((ff, d), lambda i: (0, 0)),
                  vec_spec, vec_spec],
        out_specs=[row_spec, row_spec],
        out_shape=[jax.ShapeDtypeStruct((t, d), F32), jax.ShapeDtypeStruct((t, d), BF16)],
        scratch_shapes=[pltpu.VMEM((TOP_K, tm, d), F32), pltpu.SemaphoreType.DMA(())],
        compiler_params=_cparams(("arbitrary",)),
        name="combine",
    )(pos, os_, gate_t, x, xb, sh_wgu, sh_wd, g.reshape(1, d).astype(F32), b.reshape(1, d).astype(F32))


def _moe(x, xb, router_w, router_b, w_gate, w_up, w_down, sh_wg, sh_wu, sh_wd, g, b, *, alpha):
    t, d = x.shape
    bm = GMM_BLOCK
    eidx, gate, rank, counts = _router(x, router_w, router_b, tm=512)
    counts = counts.reshape(N_EXPERTS)
    padded = (counts + bm - 1) // bm * bm
    pad_end = jnp.cumsum(padded)
    pad_start = pad_end - padded
    n_blocks = t * TOP_K // bm + N_EXPERTS
    blk_e = jnp.minimum(jnp.searchsorted(pad_end, jnp.arange(n_blocks, dtype=jnp.int32) * bm, side="right"),
                        N_EXPERTS - 1).astype(jnp.int32)
    nused = (pad_end[-1] // bm).astype(jnp.int32).reshape(1)
    start_of = jnp.sum(jnp.where(eidx[:, :, None] == jnp.arange(N_EXPERTS, dtype=jnp.int32), pad_start, 0), axis=-1)
    pos = (start_of + rank).astype(jnp.int32)
    xs = _dispatch(x, pos, n_blocks * bm, tm=512)
    os_ = _gmm(xs, w_gate, w_up, w_down, blk_e, nused, bm=bm)
    sh_wgu = jnp.concatenate([sh_wg, sh_wu], axis=1).astype(BF16)
    return _combine(os_, pos, gate.T, x, xb, sh_wgu, sh_wd.astype(BF16), g, b, alpha=alpha, tm=128)


def kernel(x, positions, even_w_in, even_w_out, a_lambda, a_subln_g, b_conv_w, odd_w_in, odd_w_out, c_sinks,
           ln1_g, ln1_b, ln2_g, ln2_b, router_w, router_b, exp_w_gate, exp_w_up, exp_w_down,
           sh_w_gate, sh_w_up, sh_w_down):
    bn, s_len, d = x.shape
    depth = ln1_g.shape[0]
    alpha = (2.0 * depth) ** 0.25
    outs = []
    for bi in range(bn):
        xf = x[bi]
        xb = xf
        tab_a = _rope_tables(positions[bi], A_QK_DIM, A_ROT_DIM)
        tab_c = _rope_tables(positions[bi], C_HEAD_DIM, C_ROT_DIM)
        for layer in range(depth):
            j = layer // 2
            if layer % 2 == 0:
                h = _proj_in(xb, even_w_in[j].astype(BF16), tab_a, q_width=A_QK_WIDTH, k_width=A_QK_WIDTH,
                             half=A_ROT_DIM // 2, q_scale=A_QK_DIM ** -0.5 * LOG2E, tm=1024, tn=512)
                lam_init = 0.8 - 0.6 * math.exp(-0.3 * layer)
                lv = a_lambda[j].astype(F32)
                lam = (jnp.exp(jnp.sum(lv[0] * lv[1])) - jnp.exp(jnp.sum(lv[2] * lv[3])) + lam_init).reshape(1)
                y_a = _diff_attn(h, lam, a_subln_g[j], lam_init=lam_init, tq=512, tk=1024)
                y_b = _short_conv(h, b_conv_w[j], tr=1024, tc=512)
                ys = [y_a, y_b]
                w_out = even_w_out[j]
            else:
                h = _proj_in(xb, odd_w_in[j].astype(BF16), tab_c, q_width=C_Q_WIDTH, k_width=C_KV_WIDTH,
                             half=C_ROT_DIM // 2, q_scale=C_HEAD_DIM ** -0.5 * LOG2E, tm=1024, tn=512)
                ys = [_win_attn(h, c_sinks[j].astype(F32) * LOG2E, tq=256)]
                w_out = odd_w_out[j]
            xf, xb = _proj_out_ln(ys, w_out.astype(BF16), xf, ln1_g[layer], ln1_b[layer], alpha=alpha, tm=256)
            xf, xb = _moe(xf, xb, router_w[layer], router_b[layer], exp_w_gate[layer], exp_w_up[layer],
                          exp_w_down[layer], sh_w_gate[layer], sh_w_up[layer], sh_w_down[layer],
                          ln2_g[layer], ln2_b[layer], alpha=alpha)
        outs.append(xf)
    return jnp.stack(outs, axis=0)
```

```python
import functools
import math

import jax
import jax.numpy as jnp
from jax import lax
from jax.experimental import pallas as pl
from jax.experimental.pallas import tpu as pltpu

F32 = jnp.float32
BF16 = jnp.bfloat16

LANES = 128
VMEM_LIMIT_BYTES = 56 * 1024 * 1024

A_HEADS = 4
A_QK_DIM = 128
A_V_DIM = 256
A_QK_WIDTH = 1024
A_WIDTH = 1024
B_WIDTH = 1024
C_Q_HEADS = 32
C_KV_HEADS = 4
C_HEAD_DIM = 64
C_GROUP = C_Q_HEADS // C_KV_HEADS
C_Q_WIDTH = C_Q_HEADS * C_HEAD_DIM
C_KV_WIDTH = C_KV_HEADS * C_HEAD_DIM
WINDOW = 128
ROPE_THETA = 500000.0
A_ROT_DIM = A_QK_DIM // 4
C_ROT_DIM = C_HEAD_DIM // 4
N_EXPERTS = 64
TOP_K = 8
N_GROUPS = 8
TOPK_GROUPS = 4
EXPERTS_PER_GROUP = N_EXPERTS // N_GROUPS
ROUTE_SCALE = 2.5
LN_EPS = 1e-5
LOG2E = math.log2(math.e)

GMM_BLOCK = 256


def _cparams(semantics):
    return pltpu.CompilerParams(dimension_semantics=semantics, vmem_limit_bytes=VMEM_LIMIT_BYTES)


def _tile(dim, want):
    t = min(dim, want)
    assert dim % t == 0, (dim, want)
    return t


def _rope_tables(positions, head_dim, rot_dim):
    half = rot_dim // 2
    inv_freq = ROPE_THETA ** (-jnp.arange(half, dtype=F32) * 2.0 / rot_dim)
    ang = positions.astype(F32)[:, None] * inv_freq
    cos, sin = jnp.cos(ang), jnp.sin(ang)
    lane = jnp.arange(LANES) % head_dim
    idx = lane % half
    cos_l = jnp.take(cos, idx, axis=1)
    sin_l = jnp.take(sin, idx, axis=1)
    c = jnp.where(lane < 2 * half, cos_l, 1.0)
    s1 = jnp.where(lane < half, -sin_l, 0.0)
    s2 = jnp.where((lane >= half) & (lane < 2 * half), sin_l, 0.0)
    return c.astype(F32), s1.astype(F32), s2.astype(F32)


def _proj_in_kernel(x_ref, w_ref, c_ref, s1_ref, s2_ref, o_ref, *, tile_kinds, half, q_scale):
    j = pl.program_id(0)
    acc = jnp.dot(x_ref[...].astype(BF16), w_ref[...], preferred_element_type=F32)
    n_chunks = acc.shape[1] // LANES
    groups = {}
    for t, kinds in enumerate(tile_kinds):
        groups.setdefault(kinds, []).append(t)
    for kinds, tiles in groups.items():
        cond = functools.reduce(jnp.logical_or, [j == t for t in tiles])

        @pl.when(cond)
        def _(kinds=kinds):
            for c in range(n_chunks):
                r = acc[:, c * LANES:(c + 1) * LANES]
                if kinds[c] != "n":
                    r = (r * c_ref[...] + pltpu.roll(r, LANES - half, 1) * s1_ref[...]
                         + pltpu.roll(r, half, 1) * s2_ref[...])
                    if kinds[c] == "q":
                        r = r * q_scale
                o_ref[:, c * LANES:(c + 1) * LANES] = r.astype(o_ref.dtype)


def _proj_in(x, w, tables, *, q_width, k_width, half, q_scale, tm, tn):
    m, kdim = x.shape
    n = w.shape[1]
    tm = _tile(m, tm)
    tn = _tile(n, tn)

    def kind(col):
        return "q" if col < q_width else ("k" if col < q_width + k_width else "n")

    tile_kinds = tuple(tuple(kind(t * tn + c * LANES) for c in range(tn // LANES)) for t in range(n // tn))
    c, s1, s2 = tables
    tab_spec = pl.BlockSpec((tm, LANES), lambda j, i: (i, 0))
    return pl.pallas_call(
        functools.partial(_proj_in_kernel, tile_kinds=tile_kinds, half=half, q_scale=q_scale),
        grid=(n // tn, m // tm),
        in_specs=[pl.BlockSpec((tm, kdim), lambda j, i: (i, 0)),
                  pl.BlockSpec((kdim, tn), lambda j, i: (0, j)),
                  tab_spec, tab_spec, tab_spec],
        out_specs=pl.BlockSpec((tm, tn), lambda j, i: (i, j)),
        out_shape=jax.ShapeDtypeStruct((m, n), BF16),
        compiler_params=_cparams(("arbitrary", "arbitrary")),
        name="proj_in",
    )(x, w, c, s1, s2)


def _diff_attn_kernel(lam_ref, q_ref, k_ref, vt_ref, g_ref, o_ref, m_ref, l_ref, acc_ref, s_ref, *, tk, lam_init):
    n_kt = vt_ref.shape[0]
    dk = A_QK_DIM
    nt = (((1,), (1,)), ((), ()))
    m_ref[...] = jnp.full(m_ref.shape, -jnp.inf, F32)
    l_ref[...] = jnp.zeros(l_ref.shape, F32)
    acc_ref[...] = jnp.zeros(acc_ref.shape, F32)

    def scores(kk, slot):
        ks = pl.multiple_of(kk * tk, tk)
        for mi in range(2):
            kb = k_ref[pl.ds(ks, tk), mi * dk:(mi + 1) * dk]
            qm = q_ref[:, mi * dk:(mi + 1) * dk]
            s_ref[slot, mi] = lax.dot_general(kb, qm, nt, preferred_element_type=F32)

    def softmax_pv(kk, slot):
        vt = vt_ref[kk]
        for mi in range(2):
            st = s_ref[slot, mi]
            m_old = m_ref[mi]
            m_new = jnp.maximum(m_old, jnp.max(st, axis=0, keepdims=True))
            alpha = jnp.exp2(m_old - m_new)
            pt = jnp.exp2(st - m_new)
            l_ref[mi] = alpha * l_ref[mi] + jnp.sum(pt, axis=0, keepdims=True)
            acc_ref[mi] = alpha * acc_ref[mi] + jnp.dot(vt, pt.astype(BF16), preferred_element_type=F32)
            m_ref[mi] = m_new

    scores(0, 0)

    def body(kk2, carry):
        kk = kk2 * 2
        scores(kk + 1, 1)
        softmax_pv(kk, 0)
        scores(jnp.minimum(kk + 2, n_kt - 1), 0)
        softmax_pv(kk + 1, 1)
        return carry

    lax.fori_loop(0, n_kt // 2, body, 0)
    lam = lam_ref[0]
    o = acc_ref[0] / l_ref[0] - lam * (acc_ref[1] / l_ref[1])
    ms = jnp.mean(o * o, axis=0, keepdims=True)
    o = o * lax.rsqrt(ms + LN_EPS) * g_ref[...] * (1.0 - lam_init)
    o_ref[...] = o.astype(o_ref.dtype)


def _diff_attn(h, lam, subln_g, *, lam_init, tq, tk):
    s_len = h.shape[0]
    tq = _tile(s_len, tq)
    tk = _tile(s_len, tk)
    w = 2 * A_QK_DIM
    k_off = A_QK_WIDTH // w
    n_kt = s_len // tk
    vt = h[:, 2 * A_QK_WIDTH:2 * A_QK_WIDTH + A_WIDTH].reshape(n_kt, tk, A_HEADS, A_V_DIM).transpose(2, 0, 3, 1)
    return pl.pallas_call(
        functools.partial(_diff_attn_kernel, tk=tk, lam_init=lam_init),
        grid=(A_HEADS, s_len // tq),
        in_specs=[pl.BlockSpec(memory_space=pltpu.SMEM),
                  pl.BlockSpec((tq, w), lambda hh, i: (i, hh)),
                  pl.BlockSpec((s_len, w), lambda hh, i: (0, k_off + hh)),
                  pl.BlockSpec((None, n_kt, A_V_DIM, tk), lambda hh, i: (hh, 0, 0, 0)),
                  pl.BlockSpec((A_V_DIM, 1), lambda hh, i: (0, 0))],
        out_specs=pl.BlockSpec((A_V_DIM, tq), lambda hh, i: (hh, i)),
        out_shape=jax.ShapeDtypeStruct((A_WIDTH, s_len), BF16),
        scratch_shapes=[pltpu.VMEM((2, 1, tq), F32), pltpu.VMEM((2, 1, tq), F32),
                        pltpu.VMEM((2, A_V_DIM, tq), F32), pltpu.VMEM((2, 2, tk, tq), F32)],
        compiler_params=_cparams(("arbitrary", "arbitrary")),
        name="diff_attn",
    )(lam, h, h, vt, subln_g.reshape(A_V_DIM, 1).astype(F32))


HALO = 16


def _short_conv_kernel(bg_ref, cg_ref, hb_ref, cgp_ref, hbp_ref, cgn_ref, hbn_ref, w_ref, o_ref):
    i = pl.program_id(0)
    n_i = pl.num_programs(0)
    u = cg_ref[...].astype(F32) * hb_ref[...].astype(F32)
    tr = u.shape[0]
    prev_row = cgp_ref[HALO - 1:HALO, :].astype(F32) * hbp_ref[HALO - 1:HALO, :].astype(F32)
    next_row = cgn_ref[0:1, :].astype(F32) * hbn_ref[0:1, :].astype(F32)
    prev_row = jnp.where(i == 0, 0.0, prev_row)
    next_row = jnp.where(i == n_i - 1, 0.0, next_row)
    row = lax.broadcasted_iota(jnp.int32, u.shape, 0)
    u_prev = jnp.where(row == 0, prev_row, pltpu.roll(u, 1, 0))
    u_next = jnp.where(row == tr - 1, next_row, pltpu.roll(u, tr - 1, 0))
    w = w_ref[...]
    y = w[0:1, :] * u_prev + w[1:2, :] * u + w[2:3, :] * u_next
    o_ref[...] = (bg_ref[...].astype(F32) * y).astype(o_ref.dtype)


def _short_conv(h, conv_w, *, tr, tc):
    s_len = h.shape[0]
    tr = _tile(s_len, tr)
    tc = _tile(B_WIDTH, tc)
    base = 2 * A_QK_WIDTH + A_WIDTH
    bg_off, cg_off, hb_off = base // tc, (base + B_WIDTH) // tc, (base + 2 * B_WIDTH) // tc
    rb = tr // HALO
    last = s_len // HALO - 1

    def main(off):
        return pl.BlockSpec((tr, tc), lambda i, c: (i, off + c))

    def prev(off):
        return pl.BlockSpec((HALO, tc), lambda i, c: (jnp.maximum(i * rb - 1, 0), off + c))

    def nxt(off):
        return pl.BlockSpec((HALO, tc), lambda i, c: (jnp.minimum((i + 1) * rb, last), off + c))

    return pl.pallas_call(
        _short_conv_kernel,
        grid=(s_len // tr, B_WIDTH // tc),
        in_specs=[main(bg_off), main(cg_off), main(hb_off), prev(cg_off), prev(hb_off), nxt(cg_off), nxt(hb_off),
                  pl.BlockSpec((3, tc), lambda i, c: (0, c))],
        out_specs=pl.BlockSpec((tr, tc), lambda i, c: (i, c)),
        out_shape=jax.ShapeDtypeStruct((s_len, B_WIDTH), BF16),
        compiler_params=_cparams(("arbitrary", "arbitrary")),
        name="short_conv",
    )(h, h, h, h, h, h, h, conv_w.astype(F32))


def _win_attn_kernel(sink_ref, q_ref, k_ref, v_ref, o_ref, *, tq):
    i = pl.program_id(0)
    s_len = k_ref.shape[0]
    win = tq + 2 * WINDOW
    hd = C_HEAD_DIM
    nt = (((1,), (1,)), ((), ()))
    q_start = i * tq
    k_start = pl.multiple_of(jnp.clip(q_start - WINDOW, 0, s_len - win), WINDOW)
    qpos = q_start + lax.broadcasted_iota(jnp.int32, (tq, win), 0)
    kpos = k_start + lax.broadcasted_iota(jnp.int32, (tq, win), 1)
    valid = jnp.abs(qpos - kpos) <= WINDOW
    kw = k_ref[pl.ds(k_start, win), :]
    vw = v_ref[pl.ds(k_start, win), :]
    for g in range(C_KV_HEADS):
        kg = kw[:, g * hd:(g + 1) * hd]
        vg = vw[:, g * hd:(g + 1) * hd]
        for j in range(C_GROUP):
            qh = g * C_GROUP + j
            q = q_ref[:, qh * hd:(qh + 1) * hd]
            s = lax.dot_general(q, kg, nt, preferred_element_type=F32)
            s = jnp.where(valid, s, -jnp.inf)
            sink = sink_ref[qh]
            m = jnp.maximum(jnp.max(s, axis=-1, keepdims=True), sink)
            e = jnp.exp2(s - m)
            denom = jnp.sum(e, axis=-1, keepdims=True) + jnp.exp2(sink - m)
            p = e / denom
            o = jnp.dot(p.astype(BF16), vg, preferred_element_type=F32)
            o_ref[:, qh * hd:(qh + 1) * hd] = o.astype(o_ref.dtype)


def _win_attn(h, sinks_log2, *, tq):
    s_len = h.shape[0]
    tq = _tile(s_len, tq)
    assert s_len >= tq + 2 * WINDOW
    k_off = C_Q_WIDTH // C_KV_WIDTH
    return pl.pallas_call(
        functools.partial(_win_attn_kernel, tq=tq),
        grid=(s_len // tq,),
        in_specs=[pl.BlockSpec(memory_space=pltpu.SMEM),
                  pl.BlockSpec((tq, C_Q_WIDTH), lambda i: (i, 0)),
                  pl.BlockSpec((s_len, C_KV_WIDTH), lambda i: (0, k_off)),
                  pl.BlockSpec((s_len, C_KV_WIDTH), lambda i: (0, k_off + 1))],
        out_specs=pl.BlockSpec((tq, C_Q_WIDTH), lambda i: (i, 0)),
        out_shape=jax.ShapeDtypeStruct((s_len, C_Q_WIDTH), BF16),
        compiler_params=_cparams(("arbitrary",)),
        name="win_attn",
    )(sinks_log2, h, h, h)


def _layer_norm(z, g, b):
    mu = jnp.mean(z, axis=-1, keepdims=True)
    zc = z - mu
    var = jnp.mean(zc * zc, axis=-1, keepdims=True)
    return zc * lax.rsqrt(var + LN_EPS) * g + b


def _pack_bf16_pairs(z):
    n = z.shape[1] // 2
    lo = lax.bitcast_convert_type(z[:, :n].astype(jnp.bfloat16).astype(F32), jnp.int32)
    hi = lax.bitcast_convert_type(z[:, n:].astype(jnp.bfloat16).astype(F32), jnp.int32)
    return jnp.bitwise_or(hi, lax.shift_right_logical(lo, 16))


def _unpack_bf16_pairs(u):
    lo = lax.bitcast_convert_type(lax.shift_left(u, 16), F32).astype(BF16)
    hi = lax.bitcast_convert_type(jnp.bitwise_and(u, -65536), F32).astype(BF16)
    return jnp.concatenate([lo, hi], axis=1)


def _proj_out_ln_kernel(*refs, n_in, alpha):
    y_refs = refs[:n_in]
    w_ref, x_ref, g_ref, b_ref, xo_ref, xb_ref, xp_ref = refs[n_in:]
    acc = None
    row = 0
    for y_ref in y_refs:
        kw = y_ref.shape[1]
        part = jnp.dot(y_ref[...], w_ref[row:row + kw, :], preferred_element_type=F32)
        acc = part if acc is None else acc + part
        row += kw
    z = _layer_norm(alpha * x_ref[...] + acc, g_ref[...], b_ref[...])
    xo_ref[...] = z
    xb_ref[...] = z.astype(BF16)
    xp_ref[...] = _pack_bf16_pairs(z)


def _proj_out_ln(ys, w, x, g, b, *, alpha, tm):
    m, d = x.shape
    tm = _tile(m, tm)
    kdim = w.shape[0]
    in_specs = [pl.BlockSpec((tm, y.shape[1]), lambda i: (i, 0)) for y in ys]
    in_specs += [pl.BlockSpec((kdim, d), lambda i: (0, 0)),
                 pl.BlockSpec((tm, d), lambda i: (i, 0)),
                 pl.BlockSpec((1, d), lambda i: (0, 0)),
                 pl.BlockSpec((1, d), lambda i: (0, 0))]
    return pl.pallas_call(
        functools.partial(_proj_out_ln_kernel, n_in=len(ys), alpha=alpha),
        grid=(m // tm,),
        in_specs=in_specs,
        out_specs=[pl.BlockSpec((tm, d), lambda i: (i, 0)), pl.BlockSpec((tm, d), lambda i: (i, 0)),
                   pl.BlockSpec((tm, d // 2), lambda i: (i, 0))],
        out_shape=[jax.ShapeDtypeStruct((m, d), F32), jax.ShapeDtypeStruct((m, d), BF16),
                   jax.ShapeDtypeStruct((m, d // 2), jnp.int32)],
        compiler_params=_cparams(("arbitrary",)),
        name="proj_out_ln",
    )(*ys, w, x, g.reshape(1, d).astype(F32), b.reshape(1, d).astype(F32))


def _first_argmax(vals, iota, size):
    mx = jnp.max(vals, axis=0, keepdims=True)
    idx = jnp.min(jnp.where(vals == mx, iota, size), axis=0, keepdims=True)
    return mx, idx


def _router_kernel(x_ref, wt_ref, b_ref, eidx_ref, gate_ref, rank_ref, cnt_ref, carry_ref):
    i = pl.program_id(0)
    tm = x_ref.shape[0]
    neg = -jnp.inf

    @pl.when(i == 0)
    def _():
        carry_ref[...] = jnp.zeros(carry_ref.shape, F32)

    logits = lax.dot_general(wt_ref[...], x_ref[...], (((1,), (1,)), ((), ())),
                             precision=lax.Precision.HIGHEST, preferred_element_type=F32)
    scores = jax.nn.sigmoid(logits)
    choice = scores + b_ref[...]
    iota_m = lax.broadcasted_iota(jnp.int32, (EXPERTS_PER_GROUP, tm), 0)
    gs_rows = []
    for g in range(N_GROUPS):
        cg = choice[g * EXPERTS_PER_GROUP:(g + 1) * EXPERTS_PER_GROUP, :]
        m1, i1 = _first_argmax(cg, iota_m, EXPERTS_PER_GROUP)
        m2 = jnp.max(jnp.where(iota_m == i1, neg, cg), axis=0, keepdims=True)
        gs_rows.append(m1 + m2)
    gs = jnp.concatenate(gs_rows, axis=0)
    iota_g = lax.broadcasted_iota(jnp.int32, (N_GROUPS, tm), 0)
    sel = jnp.zeros((N_GROUPS, tm), F32)
    for _ in range(TOPK_GROUPS):
        _, gi = _first_argmax(gs, iota_g, N_GROUPS)
        hit = iota_g == gi
        sel = jnp.where(hit, 1.0, sel)
        gs = jnp.where(hit, neg, gs)
    allowed = jnp.concatenate(
        [jnp.broadcast_to(sel[g:g + 1, :], (EXPERTS_PER_GROUP, tm)) for g in range(N_GROUPS)], axis=0)
    iota_e = lax.broadcasted_iota(jnp.int32, (N_EXPERTS, tm), 0)
    masked = jnp.where(allowed > 0.0, choice, neg)
    onehot = jnp.zeros((N_EXPERTS, tm), F32)
    e_rows, w_rows = [], []
    for _ in range(TOP_K):
        _, ei = _first_argmax(masked, iota_e, N_EXPERTS)
        hit = iota_e == ei
        e_rows.append(ei)
        w_rows.append(jnp.sum(jnp.where(hit, scores, 0.0), axis=0, keepdims=True))
        onehot = jnp.where(hit, 1.0, onehot)
        masked = jnp.where(hit, neg, masked)
    wsel = jnp.concatenate(w_rows, axis=0)
    gate_ref[...] = wsel / jnp.sum(wsel, axis=0, keepdims=True) * ROUTE_SCALE
    eidx_ref[...] = jnp.concatenate(e_rows, axis=0)
    tri = (lax.broadcasted_iota(jnp.int32, (tm, tm), 0) < lax.broadcasted_iota(jnp.int32, (tm, tm), 1))
    prefix = jnp.dot(onehot.astype(BF16), tri.astype(BF16), preferred_element_type=F32)
    base = prefix + carry_ref[...]
    r_rows = [jnp.sum(jnp.where(iota_e == ei, base, 0.0), axis=0, keepdims=True) for ei in e_rows]
    rank_ref[...] = jnp.concatenate(r_rows, axis=0).astype(jnp.int32)
    carry_ref[...] = carry_ref[...] + jnp.sum(onehot, axis=1, keepdims=True)
    cnt_ref[...] = carry_ref[...].astype(jnp.int32)


def _router(x, router_w, router_b, *, tm):
    t, d = x.shape
    tm = _tile(t, tm)
    wt = router_w.astype(F32).T
    kt_spec = pl.BlockSpec((TOP_K, tm), lambda i: (0, i))
    return pl.pallas_call(
        _router_kernel,
        grid=(t // tm,),
        in_specs=[pl.BlockSpec((tm, d), lambda i: (i, 0)),
                  pl.BlockSpec((N_EXPERTS, d), lambda i: (0, 0)),
                  pl.BlockSpec((N_EXPERTS, 1), lambda i: (0, 0))],
        out_specs=[kt_spec, kt_spec, kt_spec, pl.BlockSpec((N_EXPERTS, 1), lambda i: (0, 0))],
        out_shape=[jax.ShapeDtypeStruct((TOP_K, t), jnp.int32), jax.ShapeDtypeStruct((TOP_K, t), F32),
                   jax.ShapeDtypeStruct((TOP_K, t), jnp.int32), jax.ShapeDtypeStruct((N_EXPERTS, 1), jnp.int32)],
        scratch_shapes=[pltpu.VMEM((N_EXPERTS, 1), F32)],
        compiler_params=_cparams(("arbitrary",)),
        name="router",
    )(x, wt, router_b.astype(F32).reshape(N_EXPERTS, 1))


def _dispatch_kernel(pos_ref, x_ref, xs_hbm, sem):
    tm = x_ref.shape[0]

    def issue(t, carry):
        for k in range(TOP_K):
            p = pos_ref[k, t]
            pltpu.make_async_copy(x_ref.at[pl.ds(t, 1), :], xs_hbm.at[pl.ds(p, 1), :], sem).start()
        return carry

    lax.fori_loop(0, tm, issue, 0)
    for k in range(TOP_K):
        pltpu.make_async_copy(x_ref, xs_hbm.at[pl.ds(0, tm), :], sem).wait()


def _dispatch(x, pos, rows, *, tm):
    t, d = x.shape
    tm = _tile(t, tm)
    return pl.pallas_call(
        _dispatch_kernel,
        grid=(t // tm,),
        in_specs=[pl.BlockSpec((TOP_K, tm), lambda i: (0, i), memory_space=pltpu.SMEM),
                  pl.BlockSpec((tm, d), lambda i: (i, 0))],
        out_specs=pl.BlockSpec(memory_space=pl.ANY),
        out_shape=jax.ShapeDtypeStruct((rows, d), x.dtype),
        scratch_shapes=[pltpu.SemaphoreType.DMA(())],
        compiler_params=_cparams(("arbitrary",)),
        name="dispatch",
    )(pos, x)


def _gmm_kernel(blk_e_ref, nused_ref, xs_ref, wg_ref, wu_ref, wd_ref, o_ref, wgu_s, wd_s):
    b = pl.program_id(0)
    ff = wg_ref.shape[3]
    used = b < nused_ref[0]
    new_e = jnp.logical_or(b == 0, blk_e_ref[b] != blk_e_ref[jnp.maximum(b - 1, 0)])

    @pl.when(jnp.logical_and(used, new_e))
    def _():
        wgu_s[:, :ff] = wg_ref[0, 0].astype(BF16)
        wgu_s[:, ff:] = wu_ref[0, 0].astype(BF16)
        wd_s[...] = wd_ref[0, 0].astype(BF16)

    @pl.when(used)
    def _():
        x = _unpack_bf16_pairs(xs_ref[...])
        gu = jnp.dot(x, wgu_s[...], preferred_element_type=F32)
        gate = gu[:, :ff]
        hmid = gate * jax.nn.sigmoid(gate) * gu[:, ff:]
        o_ref[...] = jnp.dot(hmid.astype(BF16), wd_s[...], preferred_element_type=F32)


def _gmm(xs, w_gate, w_up, w_down, layer, blk_e, nused, *, bm):
    rows = xs.shape[0]
    d, ff = w_gate.shape[2], w_gate.shape[3]
    n_blocks = rows // bm

    def blk(b, nu):
        return jnp.minimum(b, nu[0] - 1)

    grid_spec = pltpu.PrefetchScalarGridSpec(
        num_scalar_prefetch=2,
        grid=(n_blocks,),
        in_specs=[pl.BlockSpec((bm, d // 2), lambda b, be, nu: (blk(b, nu), 0)),
                  pl.BlockSpec((1, 1, d, ff), lambda b, be, nu: (layer, be[b], 0, 0)),
                  pl.BlockSpec((1, 1, d, ff), lambda b, be, nu: (layer, be[b], 0, 0)),
                  pl.BlockSpec((1, 1, ff, d), lambda b, be, nu: (layer, be[b], 0, 0))],
        out_specs=pl.BlockSpec((bm, d), lambda b, be, nu: (blk(b, nu), 0)),
        scratch_shapes=[pltpu.VMEM((d, 2 * ff), BF16), pltpu.VMEM((ff, d), BF16)],
    )
    return pl.pallas_call(
        _gmm_kernel,
        grid_spec=grid_spec,
        out_shape=jax.ShapeDtypeStruct((rows, d), F32),
        compiler_params=_cparams(("arbitrary",)),
        name="gmm",
    )(blk_e, nused, xs, w_gate, w_up, w_down)


def _combine_kernel(pos_ref, os_hbm, gate_ref, x_ref, xb_ref, wgu_ref, wd_ref, g_ref, b_ref,
                    xo_ref, xbo_ref, buf, sem, *, alpha):
    tm = x_ref.shape[0]
    ff = wd_ref.shape[0]

    def issue(t, carry):
        for k in range(TOP_K):
            p = pos_ref[k, t]
            pltpu.make_async_copy(os_hbm.at[pl.ds(p, 1), :], buf.at[k, pl.ds(t, 1), :], sem).start()
        return carry

    lax.fori_loop(0, tm, issue, 0)
    gu = jnp.dot(xb_ref[...], wgu_ref[...], preferred_element_type=F32)
    gate = gu[:, :ff]
    hmid = gate * jax.nn.sigmoid(gate) * gu[:, ff:]
    y = jnp.dot(hmid.astype(BF16), wd_ref[...], preferred_element_type=F32)
    for k in range(TOP_K):
        pltpu.make_async_copy(os_hbm.at[pl.ds(0, tm), :], buf.at[k], sem).wait()
    routed = buf[0] * gate_ref[:, 0:1]
    for k in range(1, TOP_K):
        routed = routed + buf[k] * gate_ref[:, k:k + 1]
    z = _layer_norm(alpha * x_ref[...] + (routed + y), g_ref[...], b_ref[...])
    xo_ref[...] = z
    xbo_ref[...] = z.astype(BF16)


def _combine(os_, pos, gate_t, x, xb, sh_wgu, sh_wd, g, b, *, alpha, tm):
    t, d = x.shape
    tm = _tile(t, tm)
    ff = sh_wd.shape[0]
    row_spec = pl.BlockSpec((tm, d), lambda i: (i, 0))
    vec_spec = pl.BlockSpec((1, d), lambda i: (0, 0))
    return pl.pallas_call(
        functools.partial(_combine_kernel, alpha=alpha),
        grid=(t // tm,),
        in_specs=[pl.BlockSpec((TOP_K, tm), lambda i: (0, i), memory_space=pltpu.SMEM),
                  pl.BlockSpec(memory_space=pl.ANY),
                  pl.BlockSpec((tm, TOP_K), lambda i: (i, 0)),
                  row_spec, row_spec,
                  pl.BlockSpec((d, 2 * ff), lambda i: (0, 0)),
                  pl.BlockSpec((ff, d), lambda i: (0, 0)),
                  vec_spec, vec_spec],
        out_specs=[row_spec, row_spec],
        out_shape=[jax.ShapeDtypeStruct((t, d), F32), jax.ShapeDtypeStruct((t, d), BF16)],
        scratch_shapes=[pltpu.VMEM((TOP_K, tm, d), F32), pltpu.SemaphoreType.DMA(())],
        compiler_params=_cparams(("arbitrary",)),
        name="combine",
    )(pos, os_, gate_t, x, xb, sh_wgu, sh_wd, g.reshape(1, d).astype(F32), b.reshape(1, d).astype(F32))


def _moe(x, xb, xp, router_w, router_b, w_gate, w_up, w_down, layer, sh_wg, sh_wu, sh_wd, g, b, *, alpha):
    t, d = x.shape
    bm = GMM_BLOCK
    eidx, gate, rank, counts = _router(x, router_w, router_b, tm=512)
    counts = counts.reshape(N_EXPERTS)
    padded = (counts + bm - 1) // bm * bm
    pad_end = jnp.cumsum(padded)
    pad_start = pad_end - padded
    n_blocks = t * TOP_K // bm + N_EXPERTS
    nused = (pad_end[-1] // bm).astype(jnp.int32).reshape(1)
    blk_start = jnp.minimum(jnp.arange(n_blocks, dtype=jnp.int32), nused - 1) * bm
    blk_e = jnp.sum((pad_end[None, :] <= blk_start[:, None]).astype(jnp.int32), axis=1)
    blk_e = jnp.minimum(blk_e, N_EXPERTS - 1).astype(jnp.int32)
    start_of = jnp.sum(jnp.where(eidx[:, :, None] == jnp.arange(N_EXPERTS, dtype=jnp.int32), pad_start, 0), axis=-1)
    pos = (start_of + rank).astype(jnp.int32)
    xs = _dispatch(xp, pos, n_blocks * bm, tm=512)
    os_ = _gmm(xs, w_gate, w_up, w_down, layer, blk_e, nused, bm=bm)
    sh_wgu = jnp.concatenate([sh_wg, sh_wu], axis=1).astype(BF16)
    return _combine(os_, pos, gate.T, x, xb, sh_wgu, sh_wd.astype(BF16), g, b, alpha=alpha, tm=128)


def kernel(x, positions, even_w_in, even_w_out, a_lambda, a_subln_g, b_conv_w, odd_w_in, odd_w_out, c_sinks,
           ln1_g, ln1_b, ln2_g, ln2_b, router_w, router_b, exp_w_gate, exp_w_up, exp_w_down,
           sh_w_gate, sh_w_up, sh_w_down):
    bn, s_len, d = x.shape
    depth = ln1_g.shape[0]
    alpha = (2.0 * depth) ** 0.25
    outs = []
    for bi in range(bn):
        xf = x[bi]
        xb = xf
        tab_a = _rope_tables(positions[bi], A_QK_DIM, A_ROT_DIM)
        tab_c = _rope_tables(positions[bi], C_HEAD_DIM, C_ROT_DIM)
        for layer in range(depth):
            j = layer // 2
            if layer % 2 == 0:
                h = _proj_in(xb, even_w_in[j].astype(BF16), tab_a, q_width=A_QK_WIDTH, k_width=A_QK_WIDTH,
                             half=A_ROT_DIM // 2, q_scale=A_QK_DIM ** -0.5 * LOG2E, tm=1024, tn=512)
                lam_init = 0.8 - 0.6 * math.exp(-0.3 * layer)
                lv = a_lambda[j].astype(F32)
                lam = (jnp.exp(jnp.sum(lv[0] * lv[1])) - jnp.exp(jnp.sum(lv[2] * lv[3])) + lam_init).reshape(1)
                y_a = _diff_attn(h, lam, a_subln_g[j], lam_init=lam_init, tq=512, tk=512).T
                y_b = _short_conv(h, b_conv_w[j], tr=1024, tc=512)
                ys = [y_a, y_b]
                w_out = even_w_out[j]
            else:
                h = _proj_in(xb, odd_w_in[j].astype(BF16), tab_c, q_width=C_Q_WIDTH, k_width=C_KV_WIDTH,
                             half=C_ROT_DIM // 2, q_scale=C_HEAD_DIM ** -0.5 * LOG2E, tm=1024, tn=512)
                ys = [_win_attn(h, c_sinks[j].astype(F32) * LOG2E, tq=256)]
                w_out = odd_w_out[j]
            xf, xb, xp = _proj_out_ln(ys, w_out.astype(BF16), xf, ln1_g[layer], ln1_b[layer], alpha=alpha, tm=256)
            xf, xb = _moe(xf, xb, xp, router_w[layer], router_b[layer], exp_w_gate, exp_w_up, exp_w_down, layer,
                          sh_w_gate[layer], sh_w_up[layer], sh_w_down[layer],
                          ln2_g[layer], ln2_b[layer], alpha=alpha)
        outs.append(xf)
    return jnp.stack(outs, axis=0)
```

```python
import functools
import math

import jax
import jax.numpy as jnp
from jax import lax
from jax.experimental import pallas as pl
from jax.experimental.pallas import tpu as pltpu

F32 = jnp.float32
BF16 = jnp.bfloat16

LANES = 128
VMEM_LIMIT_BYTES = 56 * 1024 * 1024

A_HEADS = 4
A_QK_DIM = 128
A_V_DIM = 256
A_QK_WIDTH = 1024
A_WIDTH = 1024
B_WIDTH = 1024
C_Q_HEADS = 32
C_KV_HEADS = 4
C_HEAD_DIM = 64
C_GROUP = C_Q_HEADS // C_KV_HEADS
C_Q_WIDTH = C_Q_HEADS * C_HEAD_DIM
C_KV_WIDTH = C_KV_HEADS * C_HEAD_DIM
WINDOW = 128
ROPE_THETA = 500000.0
A_ROT_DIM = A_QK_DIM // 4
C_ROT_DIM = C_HEAD_DIM // 4
N_EXPERTS = 64
TOP_K = 8
N_GROUPS = 8
TOPK_GROUPS = 4
EXPERTS_PER_GROUP = N_EXPERTS // N_GROUPS
ROUTE_SCALE = 2.5
LN_EPS = 1e-5
LOG2E = math.log2(math.e)

GMM_BLOCK = 256


def _cparams(semantics):
    return pltpu.CompilerParams(dimension_semantics=semantics, vmem_limit_bytes=VMEM_LIMIT_BYTES)


def _tile(dim, want):
    t = min(dim, want)
    assert dim % t == 0, (dim, want)
    return t


def _rope_tables(positions, head_dim, rot_dim):
    half = rot_dim // 2
    inv_freq = ROPE_THETA ** (-jnp.arange(half, dtype=F32) * 2.0 / rot_dim)
    ang = positions.astype(F32)[:, None] * inv_freq
    cos, sin = jnp.cos(ang), jnp.sin(ang)
    lane = jnp.arange(LANES) % head_dim
    idx = lane % half
    cos_l = jnp.take(cos, idx, axis=1)
    sin_l = jnp.take(sin, idx, axis=1)
    c = jnp.where(lane < 2 * half, cos_l, 1.0)
    s1 = jnp.where(lane < half, -sin_l, 0.0)
    s2 = jnp.where((lane >= half) & (lane < 2 * half), sin_l, 0.0)
    return c.astype(F32), s1.astype(F32), s2.astype(F32)


def _proj_in_kernel(x_ref, w_ref, c_ref, s1_ref, s2_ref, o_ref, *, tile_kinds, half, q_scale):
    j = pl.program_id(0)
    acc = jnp.dot(x_ref[...].astype(BF16), w_ref[...], preferred_element_type=F32)
    n_chunks = acc.shape[1] // LANES
    groups = {}
    for t, kinds in enumerate(tile_kinds):
        groups.setdefault(kinds, []).append(t)
    for kinds, tiles in groups.items():
        cond = functools.reduce(jnp.logical_or, [j == t for t in tiles])

        @pl.when(cond)
        def _(kinds=kinds):
            for c in range(n_chunks):
                r = acc[:, c * LANES:(c + 1) * LANES]
                if kinds[c] != "n":
                    r = (r * c_ref[...] + pltpu.roll(r, LANES - half, 1) * s1_ref[...]
                         + pltpu.roll(r, half, 1) * s2_ref[...])
                    if kinds[c] == "q":
                        r = r * q_scale
                o_ref[:, c * LANES:(c + 1) * LANES] = r.astype(o_ref.dtype)


def _proj_in(x, w, tables, *, q_width, k_width, half, q_scale, tm, tn):
    m, kdim = x.shape
    n = w.shape[1]
    tm = _tile(m, tm)
    tn = _tile(n, tn)

    def kind(col):
        return "q" if col < q_width else ("k" if col < q_width + k_width else "n")

    tile_kinds = tuple(tuple(kind(t * tn + c * LANES) for c in range(tn // LANES)) for t in range(n // tn))
    c, s1, s2 = tables
    tab_spec = pl.BlockSpec((tm, LANES), lambda j, i: (i, 0))
    return pl.pallas_call(
        functools.partial(_proj_in_kernel, tile_kinds=tile_kinds, half=half, q_scale=q_scale),
        grid=(n // tn, m // tm),
        in_specs=[pl.BlockSpec((tm, kdim), lambda j, i: (i, 0)),
                  pl.BlockSpec((kdim, tn), lambda j, i: (0, j)),
                  tab_spec, tab_spec, tab_spec],
        out_specs=pl.BlockSpec((tm, tn), lambda j, i: (i, j)),
        out_shape=jax.ShapeDtypeStruct((m, n), BF16),
        compiler_params=_cparams(("arbitrary", "arbitrary")),
        name="proj_in",
    )(x, w, c, s1, s2)


def _diff_attn_kernel(lam_ref, q_ref, k_ref, vt_ref, g_ref, o_ref, m_ref, l_ref, acc_ref, s_ref, *, tk, lam_init):
    n_kt = vt_ref.shape[0]
    dk = A_QK_DIM
    nt = (((1,), (1,)), ((), ()))
    m_ref[...] = jnp.full(m_ref.shape, -jnp.inf, F32)
    l_ref[...] = jnp.zeros(l_ref.shape, F32)
    acc_ref[...] = jnp.zeros(acc_ref.shape, F32)

    def scores(kk, slot):
        ks = pl.multiple_of(kk * tk, tk)
        for mi in range(2):
            kb = k_ref[pl.ds(ks, tk), mi * dk:(mi + 1) * dk]
            qm = q_ref[:, mi * dk:(mi + 1) * dk]
            s_ref[slot, mi] = lax.dot_general(kb, qm, nt, preferred_element_type=F32)

    def softmax_pv(kk, slot):
        vt = vt_ref[kk]
        for mi in range(2):
            st = s_ref[slot, mi]
            m_old = m_ref[mi]
            m_new = jnp.maximum(m_old, jnp.max(st, axis=0, keepdims=True))
            alpha = jnp.exp2(m_old - m_new)
            pt = jnp.exp2(st - m_new)
            l_ref[mi] = alpha * l_ref[mi] + jnp.sum(pt, axis=0, keepdims=True)
            acc_ref[mi] = alpha * acc_ref[mi] + jnp.dot(vt, pt.astype(BF16), preferred_element_type=F32)
            m_ref[mi] = m_new

    scores(0, 0)

    def body(kk2, carry):
        kk = kk2 * 2
        scores(kk + 1, 1)
        softmax_pv(kk, 0)
        scores(jnp.minimum(kk + 2, n_kt - 1), 0)
        softmax_pv(kk + 1, 1)
        return carry

    lax.fori_loop(0, n_kt // 2, body, 0)
    lam = lam_ref[0]
    o = acc_ref[0] / l_ref[0] - lam * (acc_ref[1] / l_ref[1])
    ms = jnp.mean(o * o, axis=0, keepdims=True)
    o = o * lax.rsqrt(ms + LN_EPS) * g_ref[...] * (1.0 - lam_init)
    o_ref[...] = o.astype(o_ref.dtype)


def _diff_attn(h, lam, subln_g, *, lam_init, tq, tk):
    s_len = h.shape[0]
    tq = _tile(s_len, tq)
    tk = _tile(s_len, tk)
    w = 2 * A_QK_DIM
    k_off = A_QK_WIDTH // w
    n_kt = s_len // tk
    vt = h[:, 2 * A_QK_WIDTH:2 * A_QK_WIDTH + A_WIDTH].reshape(n_kt, tk, A_HEADS, A_V_DIM).transpose(2, 0, 3, 1)
    return pl.pallas_call(
        functools.partial(_diff_attn_kernel, tk=tk, lam_init=lam_init),
        grid=(A_HEADS, s_len // tq),
        in_specs=[pl.BlockSpec(memory_space=pltpu.SMEM),
                  pl.BlockSpec((tq, w), lambda hh, i: (i, hh)),
                  pl.BlockSpec((s_len, w), lambda hh, i: (0, k_off + hh)),
                  pl.BlockSpec((None, n_kt, A_V_DIM, tk), lambda hh, i: (hh, 0, 0, 0)),
                  pl.BlockSpec((A_V_DIM, 1), lambda hh, i: (0, 0))],
        out_specs=pl.BlockSpec((A_V_DIM, tq), lambda hh, i: (hh, i)),
        out_shape=jax.ShapeDtypeStruct((A_WIDTH, s_len), BF16),
        scratch_shapes=[pltpu.VMEM((2, 1, tq), F32), pltpu.VMEM((2, 1, tq), F32),
                        pltpu.VMEM((2, A_V_DIM, tq), F32), pltpu.VMEM((2, 2, tk, tq), F32)],
        compiler_params=_cparams(("arbitrary", "arbitrary")),
        name="diff_attn",
    )(lam, h, h, vt, subln_g.reshape(A_V_DIM, 1).astype(F32))


HALO = 16


def _short_conv_kernel(bg_ref, cg_ref, hb_ref, cgp_ref, hbp_ref, cgn_ref, hbn_ref, w_ref, o_ref):
    i = pl.program_id(0)
    n_i = pl.num_programs(0)
    u = cg_ref[...].astype(F32) * hb_ref[...].astype(F32)
    tr = u.shape[0]
    prev_row = cgp_ref[HALO - 1:HALO, :].astype(F32) * hbp_ref[HALO - 1:HALO, :].astype(F32)
    next_row = cgn_ref[0:1, :].astype(F32) * hbn_ref[0:1, :].astype(F32)
    prev_row = jnp.where(i == 0, 0.0, prev_row)
    next_row = jnp.where(i == n_i - 1, 0.0, next_row)
    row = lax.broadcasted_iota(jnp.int32, u.shape, 0)
    u_prev = jnp.where(row == 0, prev_row, pltpu.roll(u, 1, 0))
    u_next = jnp.where(row == tr - 1, next_row, pltpu.roll(u, tr - 1, 0))
    w = w_ref[...]
    y = w[0:1, :] * u_prev + w[1:2, :] * u + w[2:3, :] * u_next
    o_ref[...] = (bg_ref[...].astype(F32) * y).astype(o_ref.dtype)


def _short_conv(h, conv_w, *, tr, tc):
    s_len = h.shape[0]
    tr = _tile(s_len, tr)
    tc = _tile(B_WIDTH, tc)
    base = 2 * A_QK_WIDTH + A_WIDTH
    bg_off, cg_off, hb_off = base // tc, (base + B_WIDTH) // tc, (base + 2 * B_WIDTH) // tc
    rb = tr // HALO
    last = s_len // HALO - 1

    def main(off):
        return pl.BlockSpec((tr, tc), lambda i, c: (i, off + c))

    def prev(off):
        return pl.BlockSpec((HALO, tc), lambda i, c: (jnp.maximum(i * rb - 1, 0), off + c))

    def nxt(off):
        return pl.BlockSpec((HALO, tc), lambda i, c: (jnp.minimum((i + 1) * rb, last), off + c))

    return pl.pallas_call(
        _short_conv_kernel,
        grid=(s_len // tr, B_WIDTH // tc),
        in_specs=[main(bg_off), main(cg_off), main(hb_off), prev(cg_off), prev(hb_off), nxt(cg_off), nxt(hb_off),
                  pl.BlockSpec((3, tc), lambda i, c: (0, c))],
        out_specs=pl.BlockSpec((tr, tc), lambda i, c: (i, c)),
        out_shape=jax.ShapeDtypeStruct((s_len, B_WIDTH), BF16),
        compiler_params=_cparams(("arbitrary", "arbitrary")),
        name="short_conv",
    )(h, h, h, h, h, h, h, conv_w.astype(F32))


KEY_TILE = 128


def _win_first_tile(i, tq, n_win, n_kt):
    return jnp.clip(i * (tq // KEY_TILE) - WINDOW // KEY_TILE, 0, n_kt - n_win)


def _win_attn_kernel(sink_ref, q_ref, *refs, tq, n_win, n_kt):
    klo_refs = refs[:n_win]
    khi_refs = refs[n_win:2 * n_win]
    vt_refs = refs[2 * n_win:3 * n_win]
    o_ref, s_ref, ot_ref = refs[3 * n_win:]
    i = pl.program_id(0)
    win = n_win * KEY_TILE
    hd = C_HEAD_DIM
    nt = (((1,), (1,)), ((), ()))
    q_start = i * tq
    k_start = _win_first_tile(i, tq, n_win, n_kt) * KEY_TILE
    kpos = k_start + lax.broadcasted_iota(jnp.int32, (win, tq), 0)
    qpos = q_start + lax.broadcasted_iota(jnp.int32, (win, tq), 1)
    bias = jnp.where(jnp.abs(qpos - kpos) <= WINDOW, 0.0, -jnp.inf)
    k_lo = jnp.concatenate([r[...] for r in klo_refs], axis=0)
    k_hi = jnp.concatenate([r[...] for r in khi_refs], axis=0)
    vt = jnp.concatenate([r[...] for r in vt_refs], axis=1)
    k_sel = (k_lo, k_hi)

    def scores(qh):
        pair, half = divmod(qh, 2)
        g = qh // C_GROUP
        qp = q_ref[:, pair * LANES:(pair + 1) * LANES]
        st = lax.dot_general(k_sel[half][:, g * LANES:(g + 1) * LANES], qp, nt, preferred_element_type=F32)
        s_ref[qh % 2] = st + bias

    def softmax_pv(qh):
        pair, half = divmod(qh, 2)
        g = qh // C_GROUP
        sink = sink_ref[qh]
        st = s_ref[qh % 2]
        m = jnp.maximum(jnp.max(st, axis=0, keepdims=True), sink)
        e = jnp.exp2(st - m)
        denom = jnp.sum(e, axis=0, keepdims=True) + jnp.exp2(sink - m)
        ot_ref[half * hd:(half + 1) * hd, :] = jnp.dot(vt[g * hd:(g + 1) * hd, :], e.astype(BF16),
                                                       preferred_element_type=F32) / denom
        if half == 1:
            o_ref[:, pair * LANES:(pair + 1) * LANES] = ot_ref[...].T.astype(o_ref.dtype)

    scores(0)
    for qh in range(C_Q_HEADS):
        if qh + 1 < C_Q_HEADS:
            scores(qh + 1)
        softmax_pv(qh)


def _win_attn(h, sinks_log2, *, tq):
    s_len = h.shape[0]
    tq = _tile(s_len, tq)
    n_win = (tq + 2 * WINDOW) // KEY_TILE
    n_kt = s_len // KEY_TILE
    assert n_kt >= n_win
    hd = C_HEAD_DIM
    k = h[:, C_Q_WIDTH:C_Q_WIDTH + C_KV_WIDTH].reshape(s_len, C_KV_HEADS, hd)
    zeros = jnp.zeros_like(k)
    k_lo = jnp.concatenate([k, zeros], axis=-1).reshape(s_len, C_KV_HEADS * LANES)
    k_hi = jnp.concatenate([zeros, k], axis=-1).reshape(s_len, C_KV_HEADS * LANES)
    vt = h[:, C_Q_WIDTH + C_KV_WIDTH:].reshape(n_kt, KEY_TILE, C_KV_WIDTH).transpose(0, 2, 1)

    def k_spec(j):
        return pl.BlockSpec((KEY_TILE, C_KV_HEADS * LANES),
                            lambda i: (_win_first_tile(i, tq, n_win, n_kt) + j, 0))

    def vt_spec(j):
        return pl.BlockSpec((None, C_KV_WIDTH, KEY_TILE),
                            lambda i: (_win_first_tile(i, tq, n_win, n_kt) + j, 0, 0))

    return pl.pallas_call(
        functools.partial(_win_attn_kernel, tq=tq, n_win=n_win, n_kt=n_kt),
        grid=(s_len // tq,),
        in_specs=[pl.BlockSpec(memory_space=pltpu.SMEM),
                  pl.BlockSpec((tq, C_Q_WIDTH), lambda i: (i, 0))]
                 + [k_spec(j) for j in range(n_win)] + [k_spec(j) for j in range(n_win)]
                 + [vt_spec(j) for j in range(n_win)],
        out_specs=pl.BlockSpec((tq, C_Q_WIDTH), lambda i: (i, 0)),
        out_shape=jax.ShapeDtypeStruct((s_len, C_Q_WIDTH), BF16),
        scratch_shapes=[pltpu.VMEM((2, n_win * KEY_TILE, tq), F32), pltpu.VMEM((2 * hd, tq), F32)],
        compiler_params=_cparams(("arbitrary",)),
        name="win_attn",
    )(sinks_log2, h, *([k_lo] * n_win), *([k_hi] * n_win), *([vt] * n_win))


def _layer_norm(z, g, b):
    mu = jnp.mean(z, axis=-1, keepdims=True)
    zc = z - mu
    var = jnp.mean(zc * zc, axis=-1, keepdims=True)
    return zc * lax.rsqrt(var + LN_EPS) * g + b


def _pack_bf16_pairs(z):
    n = z.shape[1] // 2
    lo = lax.bitcast_convert_type(z[:, :n].astype(jnp.bfloat16).astype(F32), jnp.int32)
    hi = lax.bitcast_convert_type(z[:, n:].astype(jnp.bfloat16).astype(F32), jnp.int32)
    return jnp.bitwise_or(hi, lax.shift_right_logical(lo, 16))


def _unpack_bf16_pairs(u):
    lo = lax.bitcast_convert_type(lax.shift_left(u, 16), F32).astype(BF16)
    hi = lax.bitcast_convert_type(jnp.bitwise_and(u, -65536), F32).astype(BF16)
    return jnp.concatenate([lo, hi], axis=1)


def _proj_out_ln_kernel(*refs, n_in, alpha):
    y_refs = refs[:n_in]
    w_ref, x_ref, g_ref, b_ref, xo_ref, xb_ref, xp_ref = refs[n_in:]
    acc = None
    row = 0
    for y_ref in y_refs:
        kw = y_ref.shape[1]
        part = jnp.dot(y_ref[...], w_ref[row:row + kw, :], preferred_element_type=F32)
        acc = part if acc is None else acc + part
        row += kw
    z = _layer_norm(alpha * x_ref[...] + acc, g_ref[...], b_ref[...])
    xo_ref[...] = z
    xb_ref[...] = z.astype(BF16)
    xp_ref[...] = _pack_bf16_pairs(z)


def _proj_out_ln(ys, w, x, g, b, *, alpha, tm):
    m, d = x.shape
    tm = _tile(m, tm)
    kdim = w.shape[0]
    in_specs = [pl.BlockSpec((tm, y.shape[1]), lambda i: (i, 0)) for y in ys]
    in_specs += [pl.BlockSpec((kdim, d), lambda i: (0, 0)),
                 pl.BlockSpec((tm, d), lambda i: (i, 0)),
                 pl.BlockSpec((1, d), lambda i: (0, 0)),
                 pl.BlockSpec((1, d), lambda i: (0, 0))]
    return pl.pallas_call(
        functools.partial(_proj_out_ln_kernel, n_in=len(ys), alpha=alpha),
        grid=(m // tm,),
        in_specs=in_specs,
        out_specs=[pl.BlockSpec((tm, d), lambda i: (i, 0)), pl.BlockSpec((tm, d), lambda i: (i, 0)),
                   pl.BlockSpec((tm, d // 2), lambda i: (i, 0))],
        out_shape=[jax.ShapeDtypeStruct((m, d), F32), jax.ShapeDtypeStruct((m, d), BF16),
                   jax.ShapeDtypeStruct((m, d // 2), jnp.int32)],
        compiler_params=_cparams(("arbitrary",)),
        name="proj_out_ln",
    )(*ys, w, x, g.reshape(1, d).astype(F32), b.reshape(1, d).astype(F32))


def _first_argmax(vals, iota, size):
    mx = jnp.max(vals, axis=0, keepdims=True)
    idx = jnp.min(jnp.where(vals == mx, iota, size), axis=0, keepdims=True)
    return mx, idx


def _router_kernel(x_ref, wt_ref, b_ref, eidx_ref, gate_ref, rank_ref, cnt_ref, carry_ref):
    i = pl.program_id(0)
    tm = x_ref.shape[0]
    neg = -jnp.inf

    @pl.when(i == 0)
    def _():
        carry_ref[...] = jnp.zeros(carry_ref.shape, F32)

    logits = lax.dot_general(wt_ref[...], x_ref[...], (((1,), (1,)), ((), ())),
                             precision=lax.Precision.HIGHEST, preferred_element_type=F32)
    scores = jax.nn.sigmoid(logits)
    choice = scores + b_ref[...]
    iota_m = lax.broadcasted_iota(jnp.int32, (EXPERTS_PER_GROUP, tm), 0)
    gs_rows = []
    for g in range(N_GROUPS):
        cg = choice[g * EXPERTS_PER_GROUP:(g + 1) * EXPERTS_PER_GROUP, :]
        m1, i1 = _first_argmax(cg, iota_m, EXPERTS_PER_GROUP)
        m2 = jnp.max(jnp.where(iota_m == i1, neg, cg), axis=0, keepdims=True)
        gs_rows.append(m1 + m2)
    gs = jnp.concatenate(gs_rows, axis=0)
    iota_g = lax.broadcasted_iota(jnp.int32, (N_GROUPS, tm), 0)
    sel = jnp.zeros((N_GROUPS, tm), F32)
    for _ in range(TOPK_GROUPS):
        _, gi = _first_argmax(gs, iota_g, N_GROUPS)
        hit = iota_g == gi
        sel = jnp.where(hit, 1.0, sel)
        gs = jnp.where(hit, neg, gs)
    allowed = jnp.concatenate(
        [jnp.broadcast_to(sel[g:g + 1, :], (EXPERTS_PER_GROUP, tm)) for g in range(N_GROUPS)], axis=0)
    iota_e = lax.broadcasted_iota(jnp.int32, (N_EXPERTS, tm), 0)
    masked = jnp.where(allowed > 0.0, choice, neg)
    onehot = jnp.zeros((N_EXPERTS, tm), F32)
    e_rows, w_rows = [], []
    for _ in range(TOP_K):
        _, ei = _first_argmax(masked, iota_e, N_EXPERTS)
        hit = iota_e == ei
        e_rows.append(ei)
        w_rows.append(jnp.sum(jnp.where(hit, scores, 0.0), axis=0, keepdims=True))
        onehot = jnp.where(hit, 1.0, onehot)
        masked = jnp.where(hit, neg, masked)
    wsel = jnp.concatenate(w_rows, axis=0)
    gate_ref[...] = wsel / jnp.sum(wsel, axis=0, keepdims=True) * ROUTE_SCALE
    eidx_ref[...] = jnp.concatenate(e_rows, axis=0)
    tri = (lax.broadcasted_iota(jnp.int32, (tm, tm), 0) < lax.broadcasted_iota(jnp.int32, (tm, tm), 1))
    prefix = jnp.dot(onehot.astype(BF16), tri.astype(BF16), preferred_element_type=F32)
    base = prefix + carry_ref[...]
    r_rows = [jnp.sum(jnp.where(iota_e == ei, base, 0.0), axis=0, keepdims=True) for ei in e_rows]
    rank_ref[...] = jnp.concatenate(r_rows, axis=0).astype(jnp.int32)
    carry_ref[...] = carry_ref[...] + jnp.sum(onehot, axis=1, keepdims=True)
    cnt_ref[...] = carry_ref[...].astype(jnp.int32)


def _router(x, router_w, router_b, *, tm):
    t, d = x.shape
    tm = _tile(t, tm)
    wt = router_w.astype(F32).T
    kt_spec = pl.BlockSpec((TOP_K, tm), lambda i: (0, i))
    return pl.pallas_call(
        _router_kernel,
        grid=(t // tm,),
        in_specs=[pl.BlockSpec((tm, d), lambda i: (i, 0)),
                  pl.BlockSpec((N_EXPERTS, d), lambda i: (0, 0)),
                  pl.BlockSpec((N_EXPERTS, 1), lambda i: (0, 0))],
        out_specs=[kt_spec, kt_spec, kt_spec, pl.BlockSpec((N_EXPERTS, 1), lambda i: (0, 0))],
        out_shape=[jax.ShapeDtypeStruct((TOP_K, t), jnp.int32), jax.ShapeDtypeStruct((TOP_K, t), F32),
                   jax.ShapeDtypeStruct((TOP_K, t), jnp.int32), jax.ShapeDtypeStruct((N_EXPERTS, 1), jnp.int32)],
        scratch_shapes=[pltpu.VMEM((N_EXPERTS, 1), F32)],
        compiler_params=_cparams(("arbitrary",)),
        name="router",
    )(x, wt, router_b.astype(F32).reshape(N_EXPERTS, 1))


def _dispatch_kernel(pos_ref, x_ref, xs_hbm, sem):
    tm = x_ref.shape[0]

    def issue(t, carry):
        for k in range(TOP_K):
            p = pos_ref[k, t]
            pltpu.make_async_copy(x_ref.at[pl.ds(t, 1), :], xs_hbm.at[pl.ds(p, 1), :], sem).start()
        return carry

    lax.fori_loop(0, tm, issue, 0)
    for k in range(TOP_K):
        pltpu.make_async_copy(x_ref, xs_hbm.at[pl.ds(0, tm), :], sem).wait()


def _dispatch(x, pos, rows, *, tm):
    t, d = x.shape
    tm = _tile(t, tm)
    return pl.pallas_call(
        _dispatch_kernel,
        grid=(t // tm,),
        in_specs=[pl.BlockSpec((TOP_K, tm), lambda i: (0, i), memory_space=pltpu.SMEM),
                  pl.BlockSpec((tm, d), lambda i: (i, 0))],
        out_specs=pl.BlockSpec(memory_space=pl.ANY),
        out_shape=jax.ShapeDtypeStruct((rows, d), x.dtype),
        scratch_shapes=[pltpu.SemaphoreType.DMA(())],
        compiler_params=_cparams(("arbitrary",)),
        name="dispatch",
    )(pos, x)


def _expert_weight_copies(w_hbm, stage, sems, layer, e):
    return [pltpu.make_async_copy(w.at[layer, e], st, sems.at[i]) for i, (w, st) in enumerate(zip(w_hbm, stage))]


def _gmm_kernel(blk_e_ref, nxt_e_ref, nused_ref, xs_ref, wg_hbm, wu_hbm, wd_hbm, o_ref,
                stage_g, stage_u, stage_d, wgu_s, wd_s, sems, *, layer):
    b = pl.program_id(0)
    ff = stage_g.shape[1]
    used = b < nused_ref[0]
    new_e = jnp.logical_or(b == 0, blk_e_ref[b] != blk_e_ref[jnp.maximum(b - 1, 0)])
    copies = functools.partial(_expert_weight_copies, (wg_hbm, wu_hbm, wd_hbm), (stage_g, stage_u, stage_d), sems,
                               layer)

    @pl.when(b == 0)
    def _():
        for c in copies(blk_e_ref[0]):
            c.start()

    @pl.when(jnp.logical_and(used, new_e))
    def _():
        for c in copies(blk_e_ref[b]):
            c.wait()
        wgu_s[:, :ff] = stage_g[...].astype(BF16)
        wgu_s[:, ff:] = stage_u[...].astype(BF16)
        wd_s[...] = stage_d[...].astype(BF16)

        @pl.when(nxt_e_ref[b] >= 0)
        def _():
            for c in copies(nxt_e_ref[b]):
                c.start()

    @pl.when(used)
    def _():
        x = _unpack_bf16_pairs(xs_ref[...])
        gu = jnp.dot(x, wgu_s[...], preferred_element_type=F32)
        gate = gu[:, :ff]
        hmid = gate * jax.nn.sigmoid(gate) * gu[:, ff:]
        o_ref[...] = _pack_bf16_pairs(jnp.dot(hmid.astype(BF16), wd_s[...], preferred_element_type=F32))


def _gmm(xs, w_gate, w_up, w_down, layer, blk_e, nxt_e, nused, *, bm):
    rows = xs.shape[0]
    d, ff = w_gate.shape[2], w_gate.shape[3]
    n_blocks = rows // bm

    def blk(b, be, nx, nu):
        return (jnp.minimum(b, nu[0] - 1), 0)

    grid_spec = pltpu.PrefetchScalarGridSpec(
        num_scalar_prefetch=3,
        grid=(n_blocks,),
        in_specs=[pl.BlockSpec((bm, d // 2), blk),
                  pl.BlockSpec(memory_space=pl.ANY), pl.BlockSpec(memory_space=pl.ANY),
                  pl.BlockSpec(memory_space=pl.ANY)],
        out_specs=pl.BlockSpec((bm, d // 2), blk),
        scratch_shapes=[pltpu.VMEM((d, ff), F32), pltpu.VMEM((d, ff), F32), pltpu.VMEM((ff, d), F32),
                        pltpu.VMEM((d, 2 * ff), BF16), pltpu.VMEM((ff, d), BF16),
                        pltpu.SemaphoreType.DMA((3,))],
    )
    return pl.pallas_call(
        functools.partial(_gmm_kernel, layer=layer),
        grid_spec=grid_spec,
        out_shape=jax.ShapeDtypeStruct((rows, d // 2), jnp.int32),
        compiler_params=_cparams(("arbitrary",)),
        name="gmm",
    )(blk_e, nxt_e, nused, xs, w_gate, w_up, w_down)


def _combine_kernel(pos_ref, pos_next_ref, os_hbm, gate_ref, x_ref, xb_ref, wgu_ref, wd_ref, g_ref, b_ref,
                    xo_ref, xbo_ref, buf, sems, *, alpha):
    i = pl.program_id(0)
    n_i = pl.num_programs(0)
    tm = x_ref.shape[0]
    ff = wd_ref.shape[0]
    half = buf.shape[3]
    slot = i % 2

    def issue(p_ref, s):
        def body(t, carry):
            for k in range(TOP_K):
                p = p_ref[k, t]
                pltpu.make_async_copy(os_hbm.at[pl.ds(p, 1), :], buf.at[s, k, pl.ds(t, 1), :], sems.at[s]).start()
            return carry
        lax.fori_loop(0, tm, body, 0)

    @pl.when(i == 0)
    def _():
        issue(pos_ref, 0)

    @pl.when(i + 1 < n_i)
    def _():
        issue(pos_next_ref, 1 - slot)

    gu = jnp.dot(xb_ref[...], wgu_ref[...], preferred_element_type=F32)
    gate = gu[:, :ff]
    hmid = gate * jax.nn.sigmoid(gate) * gu[:, ff:]
    y = jnp.dot(hmid.astype(BF16), wd_ref[...], preferred_element_type=F32)
    for k in range(TOP_K):
        pltpu.make_async_copy(os_hbm.at[pl.ds(0, tm), :], buf.at[slot, k], sems.at[slot]).wait()
    lo = jnp.zeros((tm, half), F32)
    hi = jnp.zeros((tm, half), F32)
    for k in range(TOP_K):
        u = buf[slot, k]
        w = gate_ref[:, k:k + 1]
        lo = lo + lax.bitcast_convert_type(lax.shift_left(u, 16), F32) * w
        hi = hi + lax.bitcast_convert_type(jnp.bitwise_and(u, -65536), F32) * w
    routed = jnp.concatenate([lo, hi], axis=1)
    z = _layer_norm(alpha * x_ref[...] + (routed + y), g_ref[...], b_ref[...])
    xo_ref[...] = z
    xbo_ref[...] = z.astype(BF16)


def _combine(os_, pos, gate_t, x, xb, sh_wgu, sh_wd, g, b, *, alpha, tm):
    t, d = x.shape
    tm = _tile(t, tm)
    ff = sh_wd.shape[0]
    n_i = t // tm
    row_spec = pl.BlockSpec((tm, d), lambda i: (i, 0))
    vec_spec = pl.BlockSpec((1, d), lambda i: (0, 0))
    return pl.pallas_call(
        functools.partial(_combine_kernel, alpha=alpha),
        grid=(n_i,),
        in_specs=[pl.BlockSpec((TOP_K, tm), lambda i: (0, i), memory_space=pltpu.SMEM),
                  pl.BlockSpec((TOP_K, tm), lambda i: (0, jnp.minimum(i + 1, n_i - 1)), memory_space=pltpu.SMEM),
                  pl.BlockSpec(memory_space=pl.ANY),
                  pl.BlockSpec((tm, TOP_K), lambda i: (i, 0)),
                  row_spec, row_spec,
                  pl.BlockSpec((d, 2 * ff), lambda i: (0, 0)),
                  pl.BlockSpec((ff, d), lambda i: (0, 0)),
                  vec_spec, vec_spec],
        out_specs=[row_spec, row_spec],
        out_shape=[jax.ShapeDtypeStruct((t, d), F32), jax.ShapeDtypeStruct((t, d), BF16)],
        scratch_shapes=[pltpu.VMEM((2, TOP_K, tm, d // 2), jnp.int32), pltpu.SemaphoreType.DMA((2,))],
        compiler_params=_cparams(("arbitrary",)),
        name="combine",
    )(pos, pos, os_, gate_t, x, xb, sh_wgu, sh_wd, g.reshape(1, d).astype(F32), b.reshape(1, d).astype(F32))


def _moe(x, xb, xp, router_w, router_b, w_gate, w_up, w_down, layer, sh_wg, sh_wu, sh_wd, g, b, *, alpha):
    t, d = x.shape
    bm = GMM_BLOCK
    eidx, gate, rank, counts = _router(x, router_w, router_b, tm=512)
    counts = counts.reshape(N_EXPERTS)
    padded = (counts + bm - 1) // bm * bm
    pad_end = jnp.cumsum(padded)
    pad_start = pad_end - padded
    n_blocks = t * TOP_K // bm + N_EXPERTS
    nused = (pad_end[-1] // bm).astype(jnp.int32).reshape(1)
    blk_start = jnp.minimum(jnp.arange(n_blocks, dtype=jnp.int32), nused - 1) * bm
    blk_e = jnp.sum((pad_end[None, :] <= blk_start[:, None]).astype(jnp.int32), axis=1)
    blk_e = jnp.minimum(blk_e, N_EXPERTS - 1).astype(jnp.int32)
    e_ids = jnp.arange(N_EXPERTS, dtype=jnp.int32)
    later_used = jnp.logical_and(e_ids[None, :] > e_ids[:, None], counts[None, :] > 0)
    next_used = jnp.min(jnp.where(later_used, e_ids[None, :], N_EXPERTS), axis=1)
    next_used = jnp.where(next_used == N_EXPERTS, -1, next_used).astype(jnp.int32)
    nxt_e = jnp.sum(jnp.where(blk_e[:, None] == e_ids[None, :], next_used[None, :], 0), axis=1).astype(jnp.int32)
    start_of = jnp.sum(jnp.where(eidx[:, :, None] == e_ids, pad_start, 0), axis=-1)
    pos = (start_of + rank).astype(jnp.int32)
    xs = _dispatch(xp, pos, n_blocks * bm, tm=512)
    os_ = _gmm(xs, w_gate, w_up, w_down, layer, blk_e, nxt_e, nused, bm=bm)
    sh_wgu = jnp.concatenate([sh_wg, sh_wu], axis=1).astype(BF16)
    return _combine(os_, pos, gate.T, x, xb, sh_wgu, sh_wd.astype(BF16), g, b, alpha=alpha, tm=128)


def kernel(x, positions, even_w_in, even_w_out, a_lambda, a_subln_g, b_conv_w, odd_w_in, odd_w_out, c_sinks,
           ln1_g, ln1_b, ln2_g, ln2_b, router_w, router_b, exp_w_gate, exp_w_up, exp_w_down,
           sh_w_gate, sh_w_up, sh_w_down):
    bn, s_len, d = x.shape
    depth = ln1_g.shape[0]
    alpha = (2.0 * depth) ** 0.25
    outs = []
    for bi in range(bn):
        xf = x[bi]
        xb = xf
        tab_a = _rope_tables(positions[bi], A_QK_DIM, A_ROT_DIM)
        tab_c = _rope_tables(positions[bi], C_HEAD_DIM, C_ROT_DIM)
        for layer in range(depth):
            j = layer // 2
            if layer % 2 == 0:
                h = _proj_in(xb, even_w_in[j].astype(BF16), tab_a, q_width=A_QK_WIDTH, k_width=A_QK_WIDTH,
                             half=A_ROT_DIM // 2, q_scale=A_QK_DIM ** -0.5 * LOG2E, tm=1024, tn=512)
                lam_init = 0.8 - 0.6 * math.exp(-0.3 * layer)
                lv = a_lambda[j].astype(F32)
                lam = (jnp.exp(jnp.sum(lv[0] * lv[1])) - jnp.exp(jnp.sum(lv[2] * lv[3])) + lam_init).reshape(1)
                y_a = _diff_attn(h, lam, a_subln_g[j], lam_init=lam_init, tq=512, tk=512).T
                y_b = _short_conv(h, b_conv_w[j], tr=1024, tc=512)
                ys = [y_a, y_b]
                w_out = even_w_out[j]
            else:
                h = _proj_in(xb, odd_w_in[j].astype(BF16), tab_c, q_width=C_Q_WIDTH, k_width=C_KV_WIDTH,
                             half=C_ROT_DIM // 2, q_scale=C_HEAD_DIM ** -0.5 * LOG2E, tm=1024, tn=512)
                ys = [_win_attn(h, c_sinks[j].astype(F32) * LOG2E, tq=256)]
                w_out = odd_w_out[j]
            xf, xb, xp = _proj_out_ln(ys, w_out.astype(BF16), xf, ln1_g[layer], ln1_b[layer], alpha=alpha, tm=256)
            xf, xb = _moe(xf, xb, xp, router_w[layer], router_b[layer], exp_w_gate, exp_w_up, exp_w_down, layer,
                          sh_w_gate[layer], sh_w_up[layer], sh_w_down[layer],
                          ln2_g[layer], ln2_b[layer], alpha=alpha)
        outs.append(xf)
    return jnp.stack(outs, axis=0)
```

```python
import functools
import math

import jax
import jax.numpy as jnp
from jax import lax
from jax.experimental import pallas as pl
from jax.experimental.pallas import tpu as pltpu

F32 = jnp.float32
BF16 = jnp.bfloat16

LANES = 128
VMEM_LIMIT_BYTES = 56 * 1024 * 1024

A_HEADS = 4
A_QK_DIM = 128
A_V_DIM = 256
A_QK_WIDTH = 1024
A_WIDTH = 1024
B_WIDTH = 1024
C_Q_HEADS = 32
C_KV_HEADS = 4
C_HEAD_DIM = 64
C_GROUP = C_Q_HEADS // C_KV_HEADS
C_Q_WIDTH = C_Q_HEADS * C_HEAD_DIM
C_KV_WIDTH = C_KV_HEADS * C_HEAD_DIM
WINDOW = 128
ROPE_THETA = 500000.0
A_ROT_DIM = A_QK_DIM // 4
C_ROT_DIM = C_HEAD_DIM // 4
N_EXPERTS = 64
TOP_K = 8
N_GROUPS = 8
TOPK_GROUPS = 4
EXPERTS_PER_GROUP = N_EXPERTS // N_GROUPS
ROUTE_SCALE = 2.5
LN_EPS = 1e-5
LOG2E = math.log2(math.e)

GMM_BLOCK = 256


def _cparams(semantics):
    return pltpu.CompilerParams(dimension_semantics=semantics, vmem_limit_bytes=VMEM_LIMIT_BYTES)


def _tile(dim, want):
    t = min(dim, want)
    assert dim % t == 0, (dim, want)
    return t


def _rope_tables(positions, head_dim, rot_dim):
    half = rot_dim // 2
    inv_freq = ROPE_THETA ** (-jnp.arange(half, dtype=F32) * 2.0 / rot_dim)
    ang = positions.astype(F32)[:, None] * inv_freq
    cos, sin = jnp.cos(ang), jnp.sin(ang)
    lane = jnp.arange(LANES) % head_dim
    idx = lane % half
    cos_l = jnp.take(cos, idx, axis=1)
    sin_l = jnp.take(sin, idx, axis=1)
    c = jnp.where(lane < 2 * half, cos_l, 1.0)
    s1 = jnp.where(lane < half, -sin_l, 0.0)
    s2 = jnp.where((lane >= half) & (lane < 2 * half), sin_l, 0.0)
    return c.astype(F32), s1.astype(F32), s2.astype(F32)


def _proj_in_kernel(x_ref, w_ref, c_ref, s1_ref, s2_ref, o_ref, *, tile_kinds, half, q_scale):
    j = pl.program_id(0)
    acc = jnp.dot(x_ref[...].astype(BF16), w_ref[...], preferred_element_type=F32)
    n_chunks = acc.shape[1] // LANES
    groups = {}
    for t, kinds in enumerate(tile_kinds):
        groups.setdefault(kinds, []).append(t)
    for kinds, tiles in groups.items():
        cond = functools.reduce(jnp.logical_or, [j == t for t in tiles])

        @pl.when(cond)
        def _(kinds=kinds):
            for c in range(n_chunks):
                r = acc[:, c * LANES:(c + 1) * LANES]
                if kinds[c] != "n":
                    r = (r * c_ref[...] + pltpu.roll(r, LANES - half, 1) * s1_ref[...]
                         + pltpu.roll(r, half, 1) * s2_ref[...])
                    if kinds[c] == "q":
                        r = r * q_scale
                o_ref[:, c * LANES:(c + 1) * LANES] = r.astype(o_ref.dtype)


def _proj_in(x, w, tables, *, q_width, k_width, half, q_scale, tm, tn):
    m, kdim = x.shape
    n = w.shape[1]
    tm = _tile(m, tm)
    tn = _tile(n, tn)

    def kind(col):
        return "q" if col < q_width else ("k" if col < q_width + k_width else "n")

    tile_kinds = tuple(tuple(kind(t * tn + c * LANES) for c in range(tn // LANES)) for t in range(n // tn))
    c, s1, s2 = tables
    tab_spec = pl.BlockSpec((tm, LANES), lambda j, i: (i, 0))
    return pl.pallas_call(
        functools.partial(_proj_in_kernel, tile_kinds=tile_kinds, half=half, q_scale=q_scale),
        grid=(n // tn, m // tm),
        in_specs=[pl.BlockSpec((tm, kdim), lambda j, i: (i, 0)),
                  pl.BlockSpec((kdim, tn), lambda j, i: (0, j)),
                  tab_spec, tab_spec, tab_spec],
        out_specs=pl.BlockSpec((tm, tn), lambda j, i: (i, j)),
        out_shape=jax.ShapeDtypeStruct((m, n), BF16),
        compiler_params=_cparams(("arbitrary", "arbitrary")),
        name="proj_in",
    )(x, w, c, s1, s2)


def _diff_attn_kernel(lam_ref, q_ref, k_ref, vt_ref, g_ref, o_ref, m_ref, l_ref, acc_ref, s_ref, *, tk, lam_init):
    n_kt = vt_ref.shape[0]
    dk = A_QK_DIM
    nt = (((1,), (1,)), ((), ()))
    m_ref[...] = jnp.full(m_ref.shape, -jnp.inf, F32)
    l_ref[...] = jnp.zeros(l_ref.shape, F32)
    acc_ref[...] = jnp.zeros(acc_ref.shape, F32)

    def scores(kk, slot):
        ks = pl.multiple_of(kk * tk, tk)
        for mi in range(2):
            kb = k_ref[pl.ds(ks, tk), mi * dk:(mi + 1) * dk]
            qm = q_ref[:, mi * dk:(mi + 1) * dk]
            s_ref[slot, mi] = lax.dot_general(kb, qm, nt, preferred_element_type=F32)

    def softmax_pv(kk, slot):
        vt = vt_ref[kk]
        for mi in range(2):
            st = s_ref[slot, mi]
            m_old = m_ref[mi]
            m_new = jnp.maximum(m_old, jnp.max(st, axis=0, keepdims=True))
            alpha = jnp.exp2(m_old - m_new)
            pt = jnp.exp2(st - m_new)
            l_ref[mi] = alpha * l_ref[mi] + jnp.sum(pt, axis=0, keepdims=True)
            acc_ref[mi] = alpha * acc_ref[mi] + jnp.dot(vt, pt.astype(BF16), preferred_element_type=F32)
            m_ref[mi] = m_new

    scores(0, 0)

    def body(kk2, carry):
        kk = kk2 * 2
        scores(kk + 1, 1)
        softmax_pv(kk, 0)
        scores(jnp.minimum(kk + 2, n_kt - 1), 0)
        softmax_pv(kk + 1, 1)
        return carry

    lax.fori_loop(0, n_kt // 2, body, 0)
    lam = lam_ref[0]
    o = acc_ref[0] / l_ref[0] - lam * (acc_ref[1] / l_ref[1])
    ms = jnp.mean(o * o, axis=0, keepdims=True)
    o = o * lax.rsqrt(ms + LN_EPS) * g_ref[...] * (1.0 - lam_init)
    o_ref[...] = o.astype(o_ref.dtype)


def _diff_attn(h, lam, subln_g, *, lam_init, tq, tk):
    s_len = h.shape[0]
    tq = _tile(s_len, tq)
    tk = _tile(s_len, tk)
    w = 2 * A_QK_DIM
    k_off = A_QK_WIDTH // w
    n_kt = s_len // tk
    vt = h[:, 2 * A_QK_WIDTH:2 * A_QK_WIDTH + A_WIDTH].reshape(n_kt, tk, A_HEADS, A_V_DIM).transpose(2, 0, 3, 1)
    return pl.pallas_call(
        functools.partial(_diff_attn_kernel, tk=tk, lam_init=lam_init),
        grid=(A_HEADS, s_len // tq),
        in_specs=[pl.BlockSpec(memory_space=pltpu.SMEM),
                  pl.BlockSpec((tq, w), lambda hh, i: (i, hh)),
                  pl.BlockSpec((s_len, w), lambda hh, i: (0, k_off + hh)),
                  pl.BlockSpec((None, n_kt, A_V_DIM, tk), lambda hh, i: (hh, 0, 0, 0)),
                  pl.BlockSpec((A_V_DIM, 1), lambda hh, i: (0, 0))],
        out_specs=pl.BlockSpec((A_V_DIM, tq), lambda hh, i: (hh, i)),
        out_shape=jax.ShapeDtypeStruct((A_WIDTH, s_len), BF16),
        scratch_shapes=[pltpu.VMEM((2, 1, tq), F32), pltpu.VMEM((2, 1, tq), F32),
                        pltpu.VMEM((2, A_V_DIM, tq), F32), pltpu.VMEM((2, 2, tk, tq), F32)],
        compiler_params=_cparams(("arbitrary", "arbitrary")),
        name="diff_attn",
    )(lam, h, h, vt, subln_g.reshape(A_V_DIM, 1).astype(F32))


HALO = 16


def _short_conv_kernel(bg_ref, cg_ref, hb_ref, cgp_ref, hbp_ref, cgn_ref, hbn_ref, w_ref, o_ref):
    i = pl.program_id(0)
    n_i = pl.num_programs(0)
    u = cg_ref[...].astype(F32) * hb_ref[...].astype(F32)
    tr = u.shape[0]
    prev_row = cgp_ref[HALO - 1:HALO, :].astype(F32) * hbp_ref[HALO - 1:HALO, :].astype(F32)
    next_row = cgn_ref[0:1, :].astype(F32) * hbn_ref[0:1, :].astype(F32)
    prev_row = jnp.where(i == 0, 0.0, prev_row)
    next_row = jnp.where(i == n_i - 1, 0.0, next_row)
    row = lax.broadcasted_iota(jnp.int32, u.shape, 0)
    u_prev = jnp.where(row == 0, prev_row, pltpu.roll(u, 1, 0))
    u_next = jnp.where(row == tr - 1, next_row, pltpu.roll(u, tr - 1, 0))
    w = w_ref[...]
    y = w[0:1, :] * u_prev + w[1:2, :] * u + w[2:3, :] * u_next
    o_ref[...] = (bg_ref[...].astype(F32) * y).astype(o_ref.dtype)


def _short_conv(h, conv_w, *, tr, tc):
    s_len = h.shape[0]
    tr = _tile(s_len, tr)
    tc = _tile(B_WIDTH, tc)
    base = 2 * A_QK_WIDTH + A_WIDTH
    bg_off, cg_off, hb_off = base // tc, (base + B_WIDTH) // tc, (base + 2 * B_WIDTH) // tc
    rb = tr // HALO
    last = s_len // HALO - 1

    def main(off):
        return pl.BlockSpec((tr, tc), lambda i, c: (i, off + c))

    def prev(off):
        return pl.BlockSpec((HALO, tc), lambda i, c: (jnp.maximum(i * rb - 1, 0), off + c))

    def nxt(off):
        return pl.BlockSpec((HALO, tc), lambda i, c: (jnp.minimum((i + 1) * rb, last), off + c))

    return pl.pallas_call(
        _short_conv_kernel,
        grid=(s_len // tr, B_WIDTH // tc),
        in_specs=[main(bg_off), main(cg_off), main(hb_off), prev(cg_off), prev(hb_off), nxt(cg_off), nxt(hb_off),
                  pl.BlockSpec((3, tc), lambda i, c: (0, c))],
        out_specs=pl.BlockSpec((tr, tc), lambda i, c: (i, c)),
        out_shape=jax.ShapeDtypeStruct((s_len, B_WIDTH), BF16),
        compiler_params=_cparams(("arbitrary", "arbitrary")),
        name="short_conv",
    )(h, h, h, h, h, h, h, conv_w.astype(F32))


KEY_TILE = 128


def _win_first_tile(i, tq, n_win, n_kt):
    return jnp.clip(i * (tq // KEY_TILE) - WINDOW // KEY_TILE, 0, n_kt - n_win)


def _win_attn_kernel(sink_ref, q_ref, *refs, tq, n_win, n_kt):
    klo_refs = refs[:n_win]
    khi_refs = refs[n_win:2 * n_win]
    vt_refs = refs[2 * n_win:3 * n_win]
    o_ref, s_ref, ot_ref = refs[3 * n_win:]
    i = pl.program_id(0)
    win = n_win * KEY_TILE
    hd = C_HEAD_DIM
    nt = (((1,), (1,)), ((), ()))
    q_start = i * tq
    k_start = _win_first_tile(i, tq, n_win, n_kt) * KEY_TILE
    kpos = k_start + lax.broadcasted_iota(jnp.int32, (win, tq), 0)
    qpos = q_start + lax.broadcasted_iota(jnp.int32, (win, tq), 1)
    bias = jnp.where(jnp.abs(qpos - kpos) <= WINDOW, 0.0, -jnp.inf)
    k_lo = jnp.concatenate([r[...] for r in klo_refs], axis=0)
    k_hi = jnp.concatenate([r[...] for r in khi_refs], axis=0)
    vt = jnp.concatenate([r[...] for r in vt_refs], axis=1)
    k_sel = (k_lo, k_hi)

    def scores(qh):
        pair, half = divmod(qh, 2)
        g = qh // C_GROUP
        qp = q_ref[:, pair * LANES:(pair + 1) * LANES]
        st = lax.dot_general(k_sel[half][:, g * LANES:(g + 1) * LANES], qp, nt, preferred_element_type=F32)
        s_ref[qh % 2] = st + bias

    def softmax_pv(qh):
        pair, half = divmod(qh, 2)
        g = qh // C_GROUP
        sink = sink_ref[qh]
        st = s_ref[qh % 2]
        m = jnp.maximum(jnp.max(st, axis=0, keepdims=True), sink)
        e = jnp.exp2(st - m)
        denom = jnp.sum(e, axis=0, keepdims=True) + jnp.exp2(sink - m)
        ot_ref[half * hd:(half + 1) * hd, :] = jnp.dot(vt[g * hd:(g + 1) * hd, :], e.astype(BF16),
                                                       preferred_element_type=F32) / denom
        if half == 1:
            o_ref[:, pair * LANES:(pair + 1) * LANES] = ot_ref[...].T.astype(o_ref.dtype)

    scores(0)
    for qh in range(C_Q_HEADS):
        if qh + 1 < C_Q_HEADS:
            scores(qh + 1)
        softmax_pv(qh)


def _win_attn(h, sinks_log2, *, tq):
    s_len = h.shape[0]
    tq = _tile(s_len, tq)
    n_win = (tq + 2 * WINDOW) // KEY_TILE
    n_kt = s_len // KEY_TILE
    assert n_kt >= n_win
    hd = C_HEAD_DIM
    k = h[:, C_Q_WIDTH:C_Q_WIDTH + C_KV_WIDTH].reshape(s_len, C_KV_HEADS, hd)
    zeros = jnp.zeros_like(k)
    k_lo = jnp.concatenate([k, zeros], axis=-1).reshape(s_len, C_KV_HEADS * LANES)
    k_hi = jnp.concatenate([zeros, k], axis=-1).reshape(s_len, C_KV_HEADS * LANES)
    vt = h[:, C_Q_WIDTH + C_KV_WIDTH:].reshape(n_kt, KEY_TILE, C_KV_WIDTH).transpose(0, 2, 1)

    def k_spec(j):
        return pl.BlockSpec((KEY_TILE, C_KV_HEADS * LANES),
                            lambda i: (_win_first_tile(i, tq, n_win, n_kt) + j, 0))

    def vt_spec(j):
        return pl.BlockSpec((None, C_KV_WIDTH, KEY_TILE),
                            lambda i: (_win_first_tile(i, tq, n_win, n_kt) + j, 0, 0))

    return pl.pallas_call(
        functools.partial(_win_attn_kernel, tq=tq, n_win=n_win, n_kt=n_kt),
        grid=(s_len // tq,),
        in_specs=[pl.BlockSpec(memory_space=pltpu.SMEM),
                  pl.BlockSpec((tq, C_Q_WIDTH), lambda i: (i, 0))]
                 + [k_spec(j) for j in range(n_win)] + [k_spec(j) for j in range(n_win)]
                 + [vt_spec(j) for j in range(n_win)],
        out_specs=pl.BlockSpec((tq, C_Q_WIDTH), lambda i: (i, 0)),
        out_shape=jax.ShapeDtypeStruct((s_len, C_Q_WIDTH), BF16),
        scratch_shapes=[pltpu.VMEM((2, n_win * KEY_TILE, tq), F32), pltpu.VMEM((2 * hd, tq), F32)],
        compiler_params=_cparams(("arbitrary",)),
        name="win_attn",
    )(sinks_log2, h, *([k_lo] * n_win), *([k_hi] * n_win), *([vt] * n_win))


def _layer_norm(z, g, b):
    mu = jnp.mean(z, axis=-1, keepdims=True)
    zc = z - mu
    var = jnp.mean(zc * zc, axis=-1, keepdims=True)
    return zc * lax.rsqrt(var + LN_EPS) * g + b


def _pack_bf16_pairs(z):
    n = z.shape[1] // 2
    lo = lax.bitcast_convert_type(z[:, :n].astype(jnp.bfloat16).astype(F32), jnp.int32)
    hi = lax.bitcast_convert_type(z[:, n:].astype(jnp.bfloat16).astype(F32), jnp.int32)
    return jnp.bitwise_or(hi, lax.shift_right_logical(lo, 16))


def _unpack_bf16_pairs(u):
    lo = lax.bitcast_convert_type(lax.shift_left(u, 16), F32).astype(BF16)
    hi = lax.bitcast_convert_type(jnp.bitwise_and(u, -65536), F32).astype(BF16)
    return jnp.concatenate([lo, hi], axis=1)


def _proj_out_ln_kernel(*refs, n_in, alpha):
    y_refs = refs[:n_in]
    w_ref, x_ref, g_ref, b_ref, xo_ref, xb_ref, xp_ref = refs[n_in:]
    acc = None
    row = 0
    for y_ref in y_refs:
        kw = y_ref.shape[1]
        part = jnp.dot(y_ref[...], w_ref[row:row + kw, :], preferred_element_type=F32)
        acc = part if acc is None else acc + part
        row += kw
    z = _layer_norm(alpha * x_ref[...] + acc, g_ref[...], b_ref[...])
    xo_ref[...] = z
    xb_ref[...] = z.astype(BF16)
    xp_ref[...] = _pack_bf16_pairs(z)


def _proj_out_ln(ys, w, x, g, b, *, alpha, tm):
    m, d = x.shape
    tm = _tile(m, tm)
    kdim = w.shape[0]
    in_specs = [pl.BlockSpec((tm, y.shape[1]), lambda i: (i, 0)) for y in ys]
    in_specs += [pl.BlockSpec((kdim, d), lambda i: (0, 0)),
                 pl.BlockSpec((tm, d), lambda i: (i, 0)),
                 pl.BlockSpec((1, d), lambda i: (0, 0)),
                 pl.BlockSpec((1, d), lambda i: (0, 0))]
    return pl.pallas_call(
        functools.partial(_proj_out_ln_kernel, n_in=len(ys), alpha=alpha),
        grid=(m // tm,),
        in_specs=in_specs,
        out_specs=[pl.BlockSpec((tm, d), lambda i: (i, 0)), pl.BlockSpec((tm, d), lambda i: (i, 0)),
                   pl.BlockSpec((tm, d // 2), lambda i: (i, 0))],
        out_shape=[jax.ShapeDtypeStruct((m, d), F32), jax.ShapeDtypeStruct((m, d), BF16),
                   jax.ShapeDtypeStruct((m, d // 2), jnp.int32)],
        compiler_params=_cparams(("arbitrary",)),
        name="proj_out_ln",
    )(*ys, w, x, g.reshape(1, d).astype(F32), b.reshape(1, d).astype(F32))


def _first_argmax(vals, iota, size):
    mx = jnp.max(vals, axis=0, keepdims=True)
    idx = jnp.min(jnp.where(vals == mx, iota, size), axis=0, keepdims=True)
    return mx, idx


def _router_kernel(x_ref, wt_ref, b_ref, eidx_ref, gate_ref, rank_ref, cnt_ref, carry_ref):
    i = pl.program_id(0)
    tm = x_ref.shape[0]
    neg = -jnp.inf

    @pl.when(i == 0)
    def _():
        carry_ref[...] = jnp.zeros(carry_ref.shape, F32)

    logits = lax.dot_general(wt_ref[...], x_ref[...], (((1,), (1,)), ((), ())),
                             precision=lax.Precision.HIGHEST, preferred_element_type=F32)
    scores = jax.nn.sigmoid(logits)
    choice = scores + b_ref[...]
    iota_m = lax.broadcasted_iota(jnp.int32, (EXPERTS_PER_GROUP, tm), 0)
    gs_rows = []
    for g in range(N_GROUPS):
        cg = choice[g * EXPERTS_PER_GROUP:(g + 1) * EXPERTS_PER_GROUP, :]
        m1, i1 = _first_argmax(cg, iota_m, EXPERTS_PER_GROUP)
        m2 = jnp.max(jnp.where(iota_m == i1, neg, cg), axis=0, keepdims=True)
        gs_rows.append(m1 + m2)
    gs = jnp.concatenate(gs_rows, axis=0)
    iota_g = lax.broadcasted_iota(jnp.int32, (N_GROUPS, tm), 0)
    sel = jnp.zeros((N_GROUPS, tm), F32)
    for _ in range(TOPK_GROUPS):
        _, gi = _first_argmax(gs, iota_g, N_GROUPS)
        hit = iota_g == gi
        sel = jnp.where(hit, 1.0, sel)
        gs = jnp.where(hit, neg, gs)
    allowed = jnp.concatenate(
        [jnp.broadcast_to(sel[g:g + 1, :], (EXPERTS_PER_GROUP, tm)) for g in range(N_GROUPS)], axis=0)
    iota_e = lax.broadcasted_iota(jnp.int32, (N_EXPERTS, tm), 0)
    masked = jnp.where(allowed > 0.0, choice, neg)
    onehot = jnp.zeros((N_EXPERTS, tm), F32)
    e_rows, w_rows = [], []
    for _ in range(TOP_K):
        _, ei = _first_argmax(masked, iota_e, N_EXPERTS)
        hit = iota_e == ei
        e_rows.append(ei)
        w_rows.append(jnp.sum(jnp.where(hit, scores, 0.0), axis=0, keepdims=True))
        onehot = jnp.where(hit, 1.0, onehot)
        masked = jnp.where(hit, neg, masked)
    wsel = jnp.concatenate(w_rows, axis=0)
    gate_ref[...] = wsel / jnp.sum(wsel, axis=0, keepdims=True) * ROUTE_SCALE
    eidx_ref[...] = jnp.concatenate(e_rows, axis=0)
    tri = (lax.broadcasted_iota(jnp.int32, (tm, tm), 0) < lax.broadcasted_iota(jnp.int32, (tm, tm), 1))
    prefix = jnp.dot(onehot.astype(BF16), tri.astype(BF16), preferred_element_type=F32)
    base = prefix + carry_ref[...]
    r_rows = [jnp.sum(jnp.where(iota_e == ei, base, 0.0), axis=0, keepdims=True) for ei in e_rows]
    rank_ref[...] = jnp.concatenate(r_rows, axis=0).astype(jnp.int32)
    carry_ref[...] = carry_ref[...] + jnp.sum(onehot, axis=1, keepdims=True)
    cnt_ref[...] = carry_ref[...].astype(jnp.int32)


def _router(x, router_w, router_b, *, tm):
    t, d = x.shape
    tm = _tile(t, tm)
    wt = router_w.astype(F32).T
    kt_spec = pl.BlockSpec((TOP_K, tm), lambda i: (0, i))
    return pl.pallas_call(
        _router_kernel,
        grid=(t // tm,),
        in_specs=[pl.BlockSpec((tm, d), lambda i: (i, 0)),
                  pl.BlockSpec((N_EXPERTS, d), lambda i: (0, 0)),
                  pl.BlockSpec((N_EXPERTS, 1), lambda i: (0, 0))],
        out_specs=[kt_spec, kt_spec, kt_spec, pl.BlockSpec((N_EXPERTS, 1), lambda i: (0, 0))],
        out_shape=[jax.ShapeDtypeStruct((TOP_K, t), jnp.int32), jax.ShapeDtypeStruct((TOP_K, t), F32),
                   jax.ShapeDtypeStruct((TOP_K, t), jnp.int32), jax.ShapeDtypeStruct((N_EXPERTS, 1), jnp.int32)],
        scratch_shapes=[pltpu.VMEM((N_EXPERTS, 1), F32)],
        compiler_params=_cparams(("arbitrary",)),
        name="router",
    )(x, wt, router_b.astype(F32).reshape(N_EXPERTS, 1))


def _dispatch_kernel(pos_ref, x_ref, xs_hbm, sem):
    tm = x_ref.shape[0]
    for t in range(tm):
        for k in range(TOP_K):
            pltpu.make_async_copy(x_ref.at[pl.ds(t, 1), :], xs_hbm.at[pl.ds(pos_ref[k, t], 1), :], sem).start()
    for k in range(TOP_K):
        pltpu.make_async_copy(x_ref, xs_hbm.at[pl.ds(0, tm), :], sem).wait()


def _dispatch(x, pos, rows, *, tm):
    t, d = x.shape
    tm = _tile(t, tm)
    return pl.pallas_call(
        _dispatch_kernel,
        grid=(t // tm,),
        in_specs=[pl.BlockSpec((TOP_K, tm), lambda i: (0, i), memory_space=pltpu.SMEM),
                  pl.BlockSpec((tm, d), lambda i: (i, 0))],
        out_specs=pl.BlockSpec(memory_space=pl.ANY),
        out_shape=jax.ShapeDtypeStruct((rows, d), x.dtype),
        scratch_shapes=[pltpu.SemaphoreType.DMA(())],
        compiler_params=_cparams(("arbitrary",)),
        name="dispatch",
    )(pos, x)


def _expert_weight_copies(w_hbm, stage, sems, layer, e):
    return [pltpu.make_async_copy(w.at[layer, e], st, sems.at[i]) for i, (w, st) in enumerate(zip(w_hbm, stage))]


def _gmm_kernel(blk_e_ref, nxt_e_ref, nused_ref, xs_ref, wg_hbm, wu_hbm, wd_hbm, o_ref,
                stage_g, stage_u, stage_d, wgu_s, wd_s, sems, *, layer):
    b = pl.program_id(0)
    ff = stage_g.shape[1]
    used = b < nused_ref[0]
    new_e = jnp.logical_or(b == 0, blk_e_ref[b] != blk_e_ref[jnp.maximum(b - 1, 0)])
    copies = functools.partial(_expert_weight_copies, (wg_hbm, wu_hbm, wd_hbm), (stage_g, stage_u, stage_d), sems,
                               layer)

    @pl.when(b == 0)
    def _():
        for c in copies(blk_e_ref[0]):
            c.start()

    @pl.when(jnp.logical_and(used, new_e))
    def _():
        for c in copies(blk_e_ref[b]):
            c.wait()
        wgu_s[:, :ff] = stage_g[...].astype(BF16)
        wgu_s[:, ff:] = stage_u[...].astype(BF16)
        wd_s[...] = stage_d[...].astype(BF16)

        @pl.when(nxt_e_ref[b] >= 0)
        def _():
            for c in copies(nxt_e_ref[b]):
                c.start()

    @pl.when(used)
    def _():
        x = _unpack_bf16_pairs(xs_ref[...])
        gu = jnp.dot(x, wgu_s[...], preferred_element_type=F32)
        gate = gu[:, :ff]
        hmid = gate * jax.nn.sigmoid(gate) * gu[:, ff:]
        o_ref[...] = _pack_bf16_pairs(jnp.dot(hmid.astype(BF16), wd_s[...], preferred_element_type=F32))


def _gmm(xs, w_gate, w_up, w_down, layer, blk_e, nxt_e, nused, *, bm):
    rows = xs.shape[0]
    d, ff = w_gate.shape[2], w_gate.shape[3]
    n_blocks = rows // bm

    def blk(b, be, nx, nu):
        return (jnp.minimum(b, nu[0] - 1), 0)

    grid_spec = pltpu.PrefetchScalarGridSpec(
        num_scalar_prefetch=3,
        grid=(n_blocks,),
        in_specs=[pl.BlockSpec((bm, d // 2), blk),
                  pl.BlockSpec(memory_space=pl.ANY), pl.BlockSpec(memory_space=pl.ANY),
                  pl.BlockSpec(memory_space=pl.ANY)],
        out_specs=pl.BlockSpec((bm, d // 2), blk),
        scratch_shapes=[pltpu.VMEM((d, ff), F32), pltpu.VMEM((d, ff), F32), pltpu.VMEM((ff, d), F32),
                        pltpu.VMEM((d, 2 * ff), BF16), pltpu.VMEM((ff, d), BF16),
                        pltpu.SemaphoreType.DMA((3,))],
    )
    return pl.pallas_call(
        functools.partial(_gmm_kernel, layer=layer),
        grid_spec=grid_spec,
        out_shape=jax.ShapeDtypeStruct((rows, d // 2), jnp.int32),
        compiler_params=_cparams(("arbitrary",)),
        name="gmm",
    )(blk_e, nxt_e, nused, xs, w_gate, w_up, w_down)


def _combine_kernel(pos_ref, pos_next_ref, os_hbm, gate_ref, x_ref, xb_ref, wgu_ref, wd_ref, g_ref, b_ref,
                    xo_ref, xbo_ref, buf_a, buf_b, y_ref, sems, *, alpha):
    i = pl.program_id(0)
    n_i = pl.num_programs(0)
    tm = buf_a.shape[1]
    ff = wd_ref.shape[0]
    half = buf_a.shape[2]
    bufs = (buf_a, buf_b)

    def row_copy(p, s, k, t):
        return pltpu.make_async_copy(os_hbm.at[pl.ds(p, 1), :], bufs[s].at[k, pl.ds(t, 1), :], sems.at[s])

    def issue(p_ref, col0, s):
        for t in range(tm):
            for k in range(TOP_K):
                row_copy(p_ref[k, col0 + t], s, k, t).start()

    def wait_slot(s):
        for k in range(TOP_K):
            pltpu.make_async_copy(os_hbm.at[pl.ds(0, tm), :], bufs[s].at[k], sems.at[s]).wait()

    def finish(r0, s):
        rows = slice(r0, r0 + tm)
        wait_slot(s)
        lo = jnp.zeros((tm, half), F32)
        hi = jnp.zeros((tm, half), F32)
        for k in range(TOP_K):
            u = bufs[s][k]
            w = gate_ref[rows, k:k + 1]
            lo = lo + lax.bitcast_convert_type(lax.shift_left(u, 16), F32) * w
            hi = hi + lax.bitcast_convert_type(jnp.bitwise_and(u, -65536), F32) * w
        routed = jnp.concatenate([lo, hi], axis=1)
        z = _layer_norm(alpha * x_ref[rows, :] + (routed + y_ref[rows, :]), g_ref[...], b_ref[...])
        xo_ref[rows, :] = z
        xbo_ref[rows, :] = z.astype(BF16)

    @pl.when(i == 0)
    def _():
        def body(t, carry):
            for k in range(TOP_K):
                row_copy(pos_ref[k, t], 0, k, t).start()
            return carry
        lax.fori_loop(0, tm, body, 0)

    issue(pos_ref, tm, 1)
    gu = jnp.dot(xb_ref[...], wgu_ref[...], preferred_element_type=F32)
    gate = gu[:, :ff]
    hmid = gate * jax.nn.sigmoid(gate) * gu[:, ff:]
    y_ref[...] = jnp.dot(hmid.astype(BF16), wd_ref[...], preferred_element_type=F32)
    finish(0, 0)
    issue(pos_next_ref, 0, 0)
    finish(tm, 1)

    @pl.when(i == n_i - 1)
    def _():
        wait_slot(0)


def _combine(os_, pos, gate_t, x, xb, sh_wgu, sh_wd, g, b, *, alpha, tm):
    t, d = x.shape
    tm = _tile(t // 2, tm)
    ff = sh_wd.shape[0]
    n_i = t // (2 * tm)
    row_spec = pl.BlockSpec((2 * tm, d), lambda i: (i, 0))
    vec_spec = pl.BlockSpec((1, d), lambda i: (0, 0))
    return pl.pallas_call(
        functools.partial(_combine_kernel, alpha=alpha),
        grid=(n_i,),
        in_specs=[pl.BlockSpec((TOP_K, 2 * tm), lambda i: (0, i), memory_space=pltpu.SMEM),
                  pl.BlockSpec((TOP_K, 2 * tm), lambda i: (0, jnp.minimum(i + 1, n_i - 1)), memory_space=pltpu.SMEM),
                  pl.BlockSpec(memory_space=pl.ANY),
                  pl.BlockSpec((2 * tm, TOP_K), lambda i: (i, 0)),
                  row_spec, row_spec,
                  pl.BlockSpec((d, 2 * ff), lambda i: (0, 0)),
                  pl.BlockSpec((ff, d), lambda i: (0, 0)),
                  vec_spec, vec_spec],
        out_specs=[row_spec, row_spec],
        out_shape=[jax.ShapeDtypeStruct((t, d), F32), jax.ShapeDtypeStruct((t, d), BF16)],
        scratch_shapes=[pltpu.VMEM((TOP_K, tm, d // 2), jnp.int32), pltpu.VMEM((TOP_K, tm, d // 2), jnp.int32),
                        pltpu.VMEM((2 * tm, d), F32), pltpu.SemaphoreType.DMA((2,))],
        compiler_params=_cparams(("arbitrary",)),
        name="combine",
    )(pos, pos, os_, gate_t, x, xb, sh_wgu, sh_wd, g.reshape(1, d).astype(F32), b.reshape(1, d).astype(F32))


def _moe(x, xb, xp, router_w, router_b, w_gate, w_up, w_down, layer, sh_wg, sh_wu, sh_wd, g, b, *, alpha):
    t, d = x.shape
    bm = GMM_BLOCK
    eidx, gate, rank, counts = _router(x, router_w, router_b, tm=512)
    counts = counts.reshape(N_EXPERTS)
    padded = (counts + bm - 1) // bm * bm
    pad_end = jnp.cumsum(padded)
    pad_start = pad_end - padded
    n_blocks = t * TOP_K // bm + N_EXPERTS
    nused = (pad_end[-1] // bm).astype(jnp.int32).reshape(1)
    blk_start = jnp.minimum(jnp.arange(n_blocks, dtype=jnp.int32), nused - 1) * bm
    blk_e = jnp.sum((pad_end[None, :] <= blk_start[:, None]).astype(jnp.int32), axis=1)
    blk_e = jnp.minimum(blk_e, N_EXPERTS - 1).astype(jnp.int32)
    e_ids = jnp.arange(N_EXPERTS, dtype=jnp.int32)
    later_used = jnp.logical_and(e_ids[None, :] > e_ids[:, None], counts[None, :] > 0)
    next_used = jnp.min(jnp.where(later_used, e_ids[None, :], N_EXPERTS), axis=1)
    next_used = jnp.where(next_used == N_EXPERTS, -1, next_used).astype(jnp.int32)
    nxt_e = jnp.sum(jnp.where(blk_e[:, None] == e_ids[None, :], next_used[None, :], 0), axis=1).astype(jnp.int32)
    start_of = jnp.sum(jnp.where(eidx[:, :, None] == e_ids, pad_start, 0), axis=-1)
    pos = (start_of + rank).astype(jnp.int32)
    xs = _dispatch(xp, pos, n_blocks * bm, tm=256)
    os_ = _gmm(xs, w_gate, w_up, w_down, layer, blk_e, nxt_e, nused, bm=bm)
    sh_wgu = jnp.concatenate([sh_wg, sh_wu], axis=1).astype(BF16)
    return _combine(os_, pos, gate.T, x, xb, sh_wgu, sh_wd.astype(BF16), g, b, alpha=alpha, tm=128)


def kernel(x, positions, even_w_in, even_w_out, a_lambda, a_subln_g, b_conv_w, odd_w_in, odd_w_out, c_sinks,
           ln1_g, ln1_b, ln2_g, ln2_b, router_w, router_b, exp_w_gate, exp_w_up, exp_w_down,
           sh_w_gate, sh_w_up, sh_w_down):
    bn, s_len, d = x.shape
    depth = ln1_g.shape[0]
    alpha = (2.0 * depth) ** 0.25
    outs = []
    for bi in range(bn):
        xf = x[bi]
        xb = xf
        tab_a = _rope_tables(positions[bi], A_QK_DIM, A_ROT_DIM)
        tab_c = _rope_tables(positions[bi], C_HEAD_DIM, C_ROT_DIM)
        for layer in range(depth):
            j = layer // 2
            if layer % 2 == 0:
                h = _proj_in(xb, even_w_in[j].astype(BF16), tab_a, q_width=A_QK_WIDTH, k_width=A_QK_WIDTH,
                             half=A_ROT_DIM // 2, q_scale=A_QK_DIM ** -0.5 * LOG2E, tm=1024, tn=512)
                lam_init = 0.8 - 0.6 * math.exp(-0.3 * layer)
                lv = a_lambda[j].astype(F32)
                lam = (jnp.exp(jnp.sum(lv[0] * lv[1])) - jnp.exp(jnp.sum(lv[2] * lv[3])) + lam_init).reshape(1)
                y_a = _diff_attn(h, lam, a_subln_g[j], lam_init=lam_init, tq=512, tk=512).T
                y_b = _short_conv(h, b_conv_w[j], tr=1024, tc=512)
                ys = [y_a, y_b]
                w_out = even_w_out[j]
            else:
                h = _proj_in(xb, odd_w_in[j].astype(BF16), tab_c, q_width=C_Q_WIDTH, k_width=C_KV_WIDTH,
                             half=C_ROT_DIM // 2, q_scale=C_HEAD_DIM ** -0.5 * LOG2E, tm=1024, tn=512)
                ys = [_win_attn(h, c_sinks[j].astype(F32) * LOG2E, tq=256)]
                w_out = odd_w_out[j]
            xf, xb, xp = _proj_out_ln(ys, w_out.astype(BF16), xf, ln1_g[layer], ln1_b[layer], alpha=alpha, tm=256)
            xf, xb = _moe(xf, xb, xp, router_w[layer], router_b[layer], exp_w_gate, exp_w_up, exp_w_down, layer,
                          sh_w_gate[layer], sh_w_up[layer], sh_w_down[layer],
                          ln2_g[layer], ln2_b[layer], alpha=alpha)
        outs.append(xf)
    return jnp.stack(outs, axis=0)
```

```python
import functools
import math

import jax
import jax.numpy as jnp
from jax import lax
from jax.experimental import pallas as pl
from jax.experimental.pallas import tpu as pltpu

F32 = jnp.float32
BF16 = jnp.bfloat16

LANES = 128
VMEM_LIMIT_BYTES = 56 * 1024 * 1024

A_HEADS = 4
A_QK_DIM = 128
A_V_DIM = 256
A_QK_WIDTH = 1024
A_WIDTH = 1024
B_WIDTH = 1024
C_Q_HEADS = 32
C_KV_HEADS = 4
C_HEAD_DIM = 64
C_GROUP = C_Q_HEADS // C_KV_HEADS
C_Q_WIDTH = C_Q_HEADS * C_HEAD_DIM
C_KV_WIDTH = C_KV_HEADS * C_HEAD_DIM
WINDOW = 128
ROPE_THETA = 500000.0
A_ROT_DIM = A_QK_DIM // 4
C_ROT_DIM = C_HEAD_DIM // 4
N_EXPERTS = 64
TOP_K = 8
N_GROUPS = 8
TOPK_GROUPS = 4
EXPERTS_PER_GROUP = N_EXPERTS // N_GROUPS
ROUTE_SCALE = 2.5
LN_EPS = 1e-5
LOG2E = math.log2(math.e)

GMM_BLOCK = 256
GMM_BLOCKS_PER_STEP = 4
CAST_ROWS = 64


def _cparams(semantics):
    return pltpu.CompilerParams(dimension_semantics=semantics, vmem_limit_bytes=VMEM_LIMIT_BYTES)


def _tile(dim, want):
    t = min(dim, want)
    assert dim % t == 0, (dim, want)
    return t


def _rope_tables(positions, head_dim, rot_dim):
    half = rot_dim // 2
    inv_freq = ROPE_THETA ** (-jnp.arange(half, dtype=F32) * 2.0 / rot_dim)
    ang = positions.astype(F32)[:, None] * inv_freq
    cos, sin = jnp.cos(ang), jnp.sin(ang)
    lane = jnp.arange(LANES) % head_dim
    idx = lane % half
    cos_l = jnp.take(cos, idx, axis=1)
    sin_l = jnp.take(sin, idx, axis=1)
    c = jnp.where(lane < 2 * half, cos_l, 1.0)
    s1 = jnp.where(lane < half, -sin_l, 0.0)
    s2 = jnp.where((lane >= half) & (lane < 2 * half), sin_l, 0.0)
    return c.astype(F32), s1.astype(F32), s2.astype(F32)


def _proj_in_kernel(x_ref, w_ref, c_ref, s1_ref, s2_ref, o_ref, wb_ref, *, tile_kinds, half, q_scale):
    j = pl.program_id(0)

    @pl.when(pl.program_id(1) == 0)
    def _():
        wb_ref[...] = w_ref[...].astype(BF16)

    acc = jnp.dot(x_ref[...].astype(BF16), wb_ref[...], preferred_element_type=F32)
    n_chunks = acc.shape[1] // LANES
    groups = {}
    for t, kinds in enumerate(tile_kinds):
        groups.setdefault(kinds, []).append(t)
    for kinds, tiles in groups.items():
        cond = functools.reduce(jnp.logical_or, [j == t for t in tiles])

        @pl.when(cond)
        def _(kinds=kinds):
            for c in range(n_chunks):
                r = acc[:, c * LANES:(c + 1) * LANES]
                if kinds[c] != "n":
                    r = (r * c_ref[...] + pltpu.roll(r, LANES - half, 1) * s1_ref[...]
                         + pltpu.roll(r, half, 1) * s2_ref[...])
                    if kinds[c] == "q":
                        r = r * q_scale
                o_ref[:, c * LANES:(c + 1) * LANES] = r.astype(o_ref.dtype)


def _proj_in(x, w_stack, w_idx, tables, *, q_width, k_width, half, q_scale, tm, tn):
    m, kdim = x.shape
    n = w_stack.shape[2]
    tm = _tile(m, tm)
    tn = _tile(n, tn)

    def kind(col):
        return "q" if col < q_width else ("k" if col < q_width + k_width else "n")

    tile_kinds = tuple(tuple(kind(t * tn + c * LANES) for c in range(tn // LANES)) for t in range(n // tn))
    c, s1, s2 = tables
    tab_spec = pl.BlockSpec((tm, LANES), lambda j, i: (i, 0))
    return pl.pallas_call(
        functools.partial(_proj_in_kernel, tile_kinds=tile_kinds, half=half, q_scale=q_scale),
        grid=(n // tn, m // tm),
        in_specs=[pl.BlockSpec((tm, kdim), lambda j, i: (i, 0)),
                  pl.BlockSpec((None, kdim, tn), lambda j, i: (w_idx, 0, j)),
                  tab_spec, tab_spec, tab_spec],
        out_specs=pl.BlockSpec((tm, tn), lambda j, i: (i, j)),
        out_shape=jax.ShapeDtypeStruct((m, n), BF16),
        scratch_shapes=[pltpu.VMEM((kdim, tn), BF16)],
        compiler_params=_cparams(("arbitrary", "arbitrary")),
        name="proj_in",
    )(x, w_stack, c, s1, s2)


def _diff_attn_kernel(lam_ref, q_ref, k_ref, vt_ref, g_ref, o_ref, m_ref, l_ref, acc_ref, s_ref, *, tk, lam_init):
    n_kt = vt_ref.shape[0]
    dk = A_QK_DIM
    nt = (((1,), (1,)), ((), ()))
    m_ref[...] = jnp.full(m_ref.shape, -jnp.inf, F32)
    l_ref[...] = jnp.zeros(l_ref.shape, F32)
    acc_ref[...] = jnp.zeros(acc_ref.shape, F32)

    def scores(kk, slot):
        ks = pl.multiple_of(kk * tk, tk)
        for mi in range(2):
            kb = k_ref[pl.ds(ks, tk), mi * dk:(mi + 1) * dk]
            qm = q_ref[:, mi * dk:(mi + 1) * dk]
            s_ref[slot, mi] = lax.dot_general(kb, qm, nt, preferred_element_type=F32)

    def softmax_pv(kk, slot):
        vt = vt_ref[kk]
        for mi in range(2):
            st = s_ref[slot, mi]
            m_old = m_ref[mi]
            m_new = jnp.maximum(m_old, jnp.max(st, axis=0, keepdims=True))
            alpha = jnp.exp2(m_old - m_new)
            pt = jnp.exp2(st - m_new)
            l_ref[mi] = alpha * l_ref[mi] + jnp.sum(pt, axis=0, keepdims=True)
            acc_ref[mi] = alpha * acc_ref[mi] + jnp.dot(vt, pt.astype(BF16), preferred_element_type=F32)
            m_ref[mi] = m_new

    scores(0, 0)

    def body(kk2, carry):
        kk = kk2 * 2
        scores(kk + 1, 1)
        softmax_pv(kk, 0)
        scores(jnp.minimum(kk + 2, n_kt - 1), 0)
        softmax_pv(kk + 1, 1)
        return carry

    lax.fori_loop(0, n_kt // 2, body, 0)
    lam = lam_ref[0]
    o = acc_ref[0] / l_ref[0] - lam * (acc_ref[1] / l_ref[1])
    ms = jnp.mean(o * o, axis=0, keepdims=True)
    o = o * lax.rsqrt(ms + LN_EPS) * g_ref[...] * (1.0 - lam_init)
    o_ref[...] = o.T.astype(o_ref.dtype)


def _diff_attn(h, lam, subln_g, *, lam_init, tq, tk):
    s_len = h.shape[0]
    tq = _tile(s_len, tq)
    tk = _tile(s_len, tk)
    w = 2 * A_QK_DIM
    k_off = A_QK_WIDTH // w
    n_kt = s_len // tk
    vt = h[:, 2 * A_QK_WIDTH:2 * A_QK_WIDTH + A_WIDTH].reshape(n_kt, tk, A_HEADS, A_V_DIM).transpose(2, 0, 3, 1)
    return pl.pallas_call(
        functools.partial(_diff_attn_kernel, tk=tk, lam_init=lam_init),
        grid=(A_HEADS, s_len // tq),
        in_specs=[pl.BlockSpec(memory_space=pltpu.SMEM),
                  pl.BlockSpec((tq, w), lambda hh, i: (i, hh)),
                  pl.BlockSpec((s_len, w), lambda hh, i: (0, k_off + hh)),
                  pl.BlockSpec((None, n_kt, A_V_DIM, tk), lambda hh, i: (hh, 0, 0, 0)),
                  pl.BlockSpec((A_V_DIM, 1), lambda hh, i: (0, 0))],
        out_specs=pl.BlockSpec((tq, A_V_DIM), lambda hh, i: (i, hh)),
        out_shape=jax.ShapeDtypeStruct((s_len, A_WIDTH), BF16),
        scratch_shapes=[pltpu.VMEM((2, 1, tq), F32), pltpu.VMEM((2, 1, tq), F32),
                        pltpu.VMEM((2, A_V_DIM, tq), F32), pltpu.VMEM((2, 2, tk, tq), F32)],
        compiler_params=_cparams(("arbitrary", "arbitrary")),
        name="diff_attn",
    )(lam, h, h, vt, subln_g.reshape(A_V_DIM, 1).astype(F32))


HALO = 16


def _short_conv_kernel(bg_ref, cg_ref, hb_ref, cgp_ref, hbp_ref, cgn_ref, hbn_ref, w_ref, o_ref):
    i = pl.program_id(0)
    n_i = pl.num_programs(0)
    u = cg_ref[...].astype(F32) * hb_ref[...].astype(F32)
    tr = u.shape[0]
    prev_row = cgp_ref[HALO - 1:HALO, :].astype(F32) * hbp_ref[HALO - 1:HALO, :].astype(F32)
    next_row = cgn_ref[0:1, :].astype(F32) * hbn_ref[0:1, :].astype(F32)
    prev_row = jnp.where(i == 0, 0.0, prev_row)
    next_row = jnp.where(i == n_i - 1, 0.0, next_row)
    row = lax.broadcasted_iota(jnp.int32, u.shape, 0)
    u_prev = jnp.where(row == 0, prev_row, pltpu.roll(u, 1, 0))
    u_next = jnp.where(row == tr - 1, next_row, pltpu.roll(u, tr - 1, 0))
    w = w_ref[...]
    y = w[0:1, :] * u_prev + w[1:2, :] * u + w[2:3, :] * u_next
    o_ref[...] = (bg_ref[...].astype(F32) * y).astype(o_ref.dtype)


def _short_conv(h, conv_w, *, tr, tc):
    s_len = h.shape[0]
    tr = _tile(s_len, tr)
    tc = _tile(B_WIDTH, tc)
    base = 2 * A_QK_WIDTH + A_WIDTH
    bg_off, cg_off, hb_off = base // tc, (base + B_WIDTH) // tc, (base + 2 * B_WIDTH) // tc
    rb = tr // HALO
    last = s_len // HALO - 1

    def main(off):
        return pl.BlockSpec((tr, tc), lambda i, c: (i, off + c))

    def prev(off):
        return pl.BlockSpec((HALO, tc), lambda i, c: (jnp.maximum(i * rb - 1, 0), off + c))

    def nxt(off):
        return pl.BlockSpec((HALO, tc), lambda i, c: (jnp.minimum((i + 1) * rb, last), off + c))

    return pl.pallas_call(
        _short_conv_kernel,
        grid=(s_len // tr, B_WIDTH // tc),
        in_specs=[main(bg_off), main(cg_off), main(hb_off), prev(cg_off), prev(hb_off), nxt(cg_off), nxt(hb_off),
                  pl.BlockSpec((3, tc), lambda i, c: (0, c))],
        out_specs=pl.BlockSpec((tr, tc), lambda i, c: (i, c)),
        out_shape=jax.ShapeDtypeStruct((s_len, B_WIDTH), BF16),
        compiler_params=_cparams(("arbitrary", "arbitrary")),
        name="short_conv",
    )(h, h, h, h, h, h, h, conv_w.astype(F32))


KEY_TILE = 128


def _win_first_tile(i, tq, n_win, n_kt):
    return jnp.clip(i * (tq // KEY_TILE) - WINDOW // KEY_TILE, 0, n_kt - n_win)


def _win_attn_kernel(sink_ref, q_ref, *refs, tq, n_win, n_kt):
    klo_refs = refs[:n_win]
    khi_refs = refs[n_win:2 * n_win]
    vt_refs = refs[2 * n_win:3 * n_win]
    o_ref, s_ref, ot_ref = refs[3 * n_win:]
    i = pl.program_id(0)
    win = n_win * KEY_TILE
    hd = C_HEAD_DIM
    nt = (((1,), (1,)), ((), ()))
    q_start = i * tq
    k_start = _win_first_tile(i, tq, n_win, n_kt) * KEY_TILE
    kpos = k_start + lax.broadcasted_iota(jnp.int32, (win, tq), 0)
    qpos = q_start + lax.broadcasted_iota(jnp.int32, (win, tq), 1)
    bias = jnp.where(jnp.abs(qpos - kpos) <= WINDOW, 0.0, -jnp.inf)
    k_lo = jnp.concatenate([r[...] for r in klo_refs], axis=0)
    k_hi = jnp.concatenate([r[...] for r in khi_refs], axis=0)
    vt = jnp.concatenate([r[...] for r in vt_refs], axis=1)
    k_sel = (k_lo, k_hi)

    def scores(qh):
        pair, half = divmod(qh, 2)
        g = qh // C_GROUP
        qp = q_ref[:, pair * LANES:(pair + 1) * LANES]
        st = lax.dot_general(k_sel[half][:, g * LANES:(g + 1) * LANES], qp, nt, preferred_element_type=F32)
        s_ref[qh % 2] = st + bias

    def softmax_pv(qh):
        pair, half = divmod(qh, 2)
        g = qh // C_GROUP
        sink = sink_ref[qh]
        st = s_ref[qh % 2]
        m = jnp.maximum(jnp.max(st, axis=0, keepdims=True), sink)
        e = jnp.exp2(st - m)
        denom = jnp.sum(e, axis=0, keepdims=True) + jnp.exp2(sink - m)
        ot_ref[half * hd:(half + 1) * hd, :] = jnp.dot(vt[g * hd:(g + 1) * hd, :], e.astype(BF16),
                                                       preferred_element_type=F32) / denom
        if half == 1:
            o_ref[:, pair * LANES:(pair + 1) * LANES] = ot_ref[...].T.astype(o_ref.dtype)

    scores(0)
    for qh in range(C_Q_HEADS):
        if qh + 1 < C_Q_HEADS:
            scores(qh + 1)
        softmax_pv(qh)


def _win_attn(h, sinks_log2, *, tq):
    s_len = h.shape[0]
    tq = _tile(s_len, tq)
    n_win = (tq + 2 * WINDOW) // KEY_TILE
    n_kt = s_len // KEY_TILE
    assert n_kt >= n_win
    hd = C_HEAD_DIM
    k = h[:, C_Q_WIDTH:C_Q_WIDTH + C_KV_WIDTH].reshape(s_len, C_KV_HEADS, hd)
    zeros = jnp.zeros_like(k)
    k_lo = jnp.concatenate([k, zeros], axis=-1).reshape(s_len, C_KV_HEADS * LANES)
    k_hi = jnp.concatenate([zeros, k], axis=-1).reshape(s_len, C_KV_HEADS * LANES)
    vt = h[:, C_Q_WIDTH + C_KV_WIDTH:].reshape(n_kt, KEY_TILE, C_KV_WIDTH).transpose(0, 2, 1)

    def k_spec(j):
        return pl.BlockSpec((KEY_TILE, C_KV_HEADS * LANES),
                            lambda i: (_win_first_tile(i, tq, n_win, n_kt) + j, 0))

    def vt_spec(j):
        return pl.BlockSpec((None, C_KV_WIDTH, KEY_TILE),
                            lambda i: (_win_first_tile(i, tq, n_win, n_kt) + j, 0, 0))

    return pl.pallas_call(
        functools.partial(_win_attn_kernel, tq=tq, n_win=n_win, n_kt=n_kt),
        grid=(s_len // tq,),
        in_specs=[pl.BlockSpec(memory_space=pltpu.SMEM),
                  pl.BlockSpec((tq, C_Q_WIDTH), lambda i: (i, 0))]
                 + [k_spec(j) for j in range(n_win)] + [k_spec(j) for j in range(n_win)]
                 + [vt_spec(j) for j in range(n_win)],
        out_specs=pl.BlockSpec((tq, C_Q_WIDTH), lambda i: (i, 0)),
        out_shape=jax.ShapeDtypeStruct((s_len, C_Q_WIDTH), BF16),
        scratch_shapes=[pltpu.VMEM((2, n_win * KEY_TILE, tq), F32), pltpu.VMEM((2 * hd, tq), F32)],
        compiler_params=_cparams(("arbitrary",)),
        name="win_attn",
    )(sinks_log2, h, *([k_lo] * n_win), *([k_hi] * n_win), *([vt] * n_win))


def _layer_norm(z, g, b):
    mu = jnp.mean(z, axis=-1, keepdims=True)
    zc = z - mu
    var = jnp.mean(zc * zc, axis=-1, keepdims=True)
    return zc * lax.rsqrt(var + LN_EPS) * g + b


def _pack_bf16_pairs(z):
    n = z.shape[1] // 2
    lo = lax.bitcast_convert_type(z[:, :n].astype(jnp.bfloat16).astype(F32), jnp.int32)
    hi = lax.bitcast_convert_type(z[:, n:].astype(jnp.bfloat16).astype(F32), jnp.int32)
    return jnp.bitwise_or(hi, lax.shift_right_logical(lo, 16))


def _unpack_bf16_pairs(u):
    lo = lax.bitcast_convert_type(lax.shift_left(u, 16), F32).astype(BF16)
    hi = lax.bitcast_convert_type(jnp.bitwise_and(u, -65536), F32).astype(BF16)
    return jnp.concatenate([lo, hi], axis=1)


def _proj_out_ln_kernel(*refs, n_in, alpha):
    y_refs = refs[:n_in]
    w_ref, x_ref, g_ref, b_ref, xo_ref, xb_ref, xp_ref = refs[n_in:]
    acc = None
    row = 0
    for y_ref in y_refs:
        kw = y_ref.shape[1]
        part = jnp.dot(y_ref[...], w_ref[row:row + kw, :], preferred_element_type=F32)
        acc = part if acc is None else acc + part
        row += kw
    z = _layer_norm(alpha * x_ref[...] + acc, g_ref[...], b_ref[...])
    xo_ref[...] = z
    xb_ref[...] = z.astype(BF16)
    xp_ref[...] = _pack_bf16_pairs(z)


def _proj_out_ln(ys, w, x, g, b, *, alpha, tm):
    m, d = x.shape
    tm = _tile(m, tm)
    kdim = w.shape[0]
    in_specs = [pl.BlockSpec((tm, y.shape[1]), lambda i: (i, 0)) for y in ys]
    in_specs += [pl.BlockSpec((kdim, d), lambda i: (0, 0)),
                 pl.BlockSpec((tm, d), lambda i: (i, 0)),
                 pl.BlockSpec((1, d), lambda i: (0, 0)),
                 pl.BlockSpec((1, d), lambda i: (0, 0))]
    return pl.pallas_call(
        functools.partial(_proj_out_ln_kernel, n_in=len(ys), alpha=alpha),
        grid=(m // tm,),
        in_specs=in_specs,
        out_specs=[pl.BlockSpec((tm, d), lambda i: (i, 0)), pl.BlockSpec((tm, d), lambda i: (i, 0)),
                   pl.BlockSpec((tm, d // 2), lambda i: (i, 0))],
        out_shape=[jax.ShapeDtypeStruct((m, d), F32), jax.ShapeDtypeStruct((m, d), BF16),
                   jax.ShapeDtypeStruct((m, d // 2), jnp.int32)],
        compiler_params=_cparams(("arbitrary",)),
        name="proj_out_ln",
    )(*ys, w, x, g.reshape(1, d).astype(F32), b.reshape(1, d).astype(F32))


def _first_argmax(vals, iota, size):
    mx = jnp.max(vals, axis=0, keepdims=True)
    idx = jnp.min(jnp.where(vals == mx, iota, size), axis=0, keepdims=True)
    return mx, idx


def _router_kernel(x_ref, wt_ref, b_ref, eidx_ref, gate_ref, rank_ref, cnt_ref, carry_ref):
    i = pl.program_id(0)
    tm = x_ref.shape[0]
    neg = -jnp.inf

    @pl.when(i == 0)
    def _():
        carry_ref[...] = jnp.zeros(carry_ref.shape, F32)

    logits = lax.dot_general(wt_ref[...], x_ref[...], (((1,), (1,)), ((), ())),
                             precision=lax.Precision.HIGHEST, preferred_element_type=F32)
    scores = jax.nn.sigmoid(logits)
    choice = scores + b_ref[...]
    iota_m = lax.broadcasted_iota(jnp.int32, (EXPERTS_PER_GROUP, tm), 0)
    gs_rows = []
    for g in range(N_GROUPS):
        cg = choice[g * EXPERTS_PER_GROUP:(g + 1) * EXPERTS_PER_GROUP, :]
        m1, i1 = _first_argmax(cg, iota_m, EXPERTS_PER_GROUP)
        m2 = jnp.max(jnp.where(iota_m == i1, neg, cg), axis=0, keepdims=True)
        gs_rows.append(m1 + m2)
    gs = jnp.concatenate(gs_rows, axis=0)
    iota_g = lax.broadcasted_iota(jnp.int32, (N_GROUPS, tm), 0)
    sel = jnp.zeros((N_GROUPS, tm), F32)
    for _ in range(TOPK_GROUPS):
        _, gi = _first_argmax(gs, iota_g, N_GROUPS)
        hit = iota_g == gi
        sel = jnp.where(hit, 1.0, sel)
        gs = jnp.where(hit, neg, gs)
    allowed = jnp.concatenate(
        [jnp.broadcast_to(sel[g:g + 1, :], (EXPERTS_PER_GROUP, tm)) for g in range(N_GROUPS)], axis=0)
    iota_e = lax.broadcasted_iota(jnp.int32, (N_EXPERTS, tm), 0)
    masked = jnp.where(allowed > 0.0, choice, neg)
    onehot = jnp.zeros((N_EXPERTS, tm), F32)
    e_rows, w_rows = [], []
    for _ in range(TOP_K):
        _, ei = _first_argmax(masked, iota_e, N_EXPERTS)
        hit = iota_e == ei
        e_rows.append(ei)
        w_rows.append(jnp.sum(jnp.where(hit, scores, 0.0), axis=0, keepdims=True))
        onehot = jnp.where(hit, 1.0, onehot)
        masked = jnp.where(hit, neg, masked)
    wsel = jnp.concatenate(w_rows, axis=0)
    gate_ref[...] = wsel / jnp.sum(wsel, axis=0, keepdims=True) * ROUTE_SCALE
    eidx_ref[...] = jnp.concatenate(e_rows, axis=0)
    tri = (lax.broadcasted_iota(jnp.int32, (tm, tm), 0) < lax.broadcasted_iota(jnp.int32, (tm, tm), 1))
    prefix = jnp.dot(onehot.astype(BF16), tri.astype(BF16), preferred_element_type=F32)
    base = prefix + carry_ref[...]
    r_rows = [jnp.sum(jnp.where(iota_e == ei, base, 0.0), axis=0, keepdims=True) for ei in e_rows]
    rank_ref[...] = jnp.concatenate(r_rows, axis=0).astype(jnp.int32)
    carry_ref[...] = carry_ref[...] + jnp.sum(onehot, axis=1, keepdims=True)
    cnt_ref[...] = carry_ref[...].astype(jnp.int32)


def _router(x, router_w, router_b, *, tm):
    t, d = x.shape
    tm = _tile(t, tm)
    wt = router_w.astype(F32).T
    kt_spec = pl.BlockSpec((TOP_K, tm), lambda i: (0, i))
    return pl.pallas_call(
        _router_kernel,
        grid=(t // tm,),
        in_specs=[pl.BlockSpec((tm, d), lambda i: (i, 0)),
                  pl.BlockSpec((N_EXPERTS, d), lambda i: (0, 0)),
                  pl.BlockSpec((N_EXPERTS, 1), lambda i: (0, 0))],
        out_specs=[kt_spec, kt_spec, kt_spec, pl.BlockSpec((N_EXPERTS, 1), lambda i: (0, 0))],
        out_shape=[jax.ShapeDtypeStruct((TOP_K, t), jnp.int32), jax.ShapeDtypeStruct((TOP_K, t), F32),
                   jax.ShapeDtypeStruct((TOP_K, t), jnp.int32), jax.ShapeDtypeStruct((N_EXPERTS, 1), jnp.int32)],
        scratch_shapes=[pltpu.VMEM((N_EXPERTS, 1), F32)],
        compiler_params=_cparams(("arbitrary",)),
        name="router",
    )(x, wt, router_b.astype(F32).reshape(N_EXPERTS, 1))


def _dispatch_kernel(pos_ref, x_ref, xs_hbm, sem):
    tm = x_ref.shape[0]
    for t in range(tm):
        for k in range(TOP_K):
            pltpu.make_async_copy(x_ref.at[pl.ds(t, 1), :], xs_hbm.at[pl.ds(pos_ref[k, t], 1), :], sem).start()
    for k in range(TOP_K):
        pltpu.make_async_copy(x_ref, xs_hbm.at[pl.ds(0, tm), :], sem).wait()


def _dispatch(x, pos, rows, *, tm):
    t, d = x.shape
    tm = _tile(t, tm)
    return pl.pallas_call(
        _dispatch_kernel,
        grid=(t // tm,),
        in_specs=[pl.BlockSpec((TOP_K, tm), lambda i: (0, i), memory_space=pltpu.SMEM),
                  pl.BlockSpec((tm, d), lambda i: (i, 0))],
        out_specs=pl.BlockSpec(memory_space=pl.ANY),
        out_shape=jax.ShapeDtypeStruct((rows, d), x.dtype),
        scratch_shapes=[pltpu.SemaphoreType.DMA(())],
        compiler_params=_cparams(("arbitrary",)),
        name="dispatch",
    )(pos, x)


def _expert_weight_copies(w_hbm, stage, sems, layer, e):
    return [pltpu.make_async_copy(w.at[layer, e], st, sems.at[i]) for i, (w, st) in enumerate(zip(w_hbm, stage))]


def _gmm_kernel(blk_e_ref, nxt_e_ref, nused_ref, xs_ref, wg_hbm, wu_hbm, wd_hbm, o_ref,
                stage_g, stage_u, stage_d, wgu_s, wd_s, sems, *, layer, bm):
    step = pl.program_id(0)
    ff = stage_g.shape[1]
    copies = functools.partial(_expert_weight_copies, (wg_hbm, wu_hbm, wd_hbm), (stage_g, stage_u, stage_d), sems,
                               layer)

    @pl.when(step == 0)
    def _():
        for c in copies(blk_e_ref[0]):
            c.start()

    for sub in range(xs_ref.shape[0] // bm):
        b = step * (xs_ref.shape[0] // bm) + sub
        rows = slice(sub * bm, (sub + 1) * bm)
        used = b < nused_ref[0]
        new_e = jnp.logical_or(b == 0, blk_e_ref[b] != blk_e_ref[jnp.maximum(b - 1, 0)])

        @pl.when(jnp.logical_and(used, new_e))
        def _(b=b):
            for c in copies(blk_e_ref[b]):
                c.wait()
            for r in range(0, stage_g.shape[0], CAST_ROWS):
                wgu_s[r:r + CAST_ROWS, :ff] = stage_g[r:r + CAST_ROWS, :].astype(BF16)
                wgu_s[r:r + CAST_ROWS, ff:] = stage_u[r:r + CAST_ROWS, :].astype(BF16)
            for r in range(0, stage_d.shape[0], CAST_ROWS):
                wd_s[r:r + CAST_ROWS, :] = stage_d[r:r + CAST_ROWS, :].astype(BF16)

            @pl.when(nxt_e_ref[b] >= 0)
            def _():
                for c in copies(nxt_e_ref[b]):
                    c.start()

        @pl.when(used)
        def _(rows=rows):
            x = _unpack_bf16_pairs(xs_ref[rows, :])
            gu = jnp.dot(x, wgu_s[...], preferred_element_type=F32)
            gate = gu[:, :ff]
            hmid = gate * jax.nn.sigmoid(gate) * gu[:, ff:]
            o_ref[rows, :] = _pack_bf16_pairs(jnp.dot(hmid.astype(BF16), wd_s[...], preferred_element_type=F32))


def _gmm(xs, w_gate, w_up, w_down, layer, blk_e, nxt_e, nused, *, bm):
    rows = xs.shape[0]
    d, ff = w_gate.shape[2], w_gate.shape[3]
    n_blocks = rows // bm
    per_step = GMM_BLOCKS_PER_STEP
    assert n_blocks % per_step == 0

    def blk(s, be, nx, nu):
        return (jnp.minimum(s, (nu[0] - 1) // per_step), 0)

    grid_spec = pltpu.PrefetchScalarGridSpec(
        num_scalar_prefetch=3,
        grid=(n_blocks // per_step,),
        in_specs=[pl.BlockSpec((per_step * bm, d // 2), blk),
                  pl.BlockSpec(memory_space=pl.ANY), pl.BlockSpec(memory_space=pl.ANY),
                  pl.BlockSpec(memory_space=pl.ANY)],
        out_specs=pl.BlockSpec((per_step * bm, d // 2), blk),
        scratch_shapes=[pltpu.VMEM((d, ff), F32), pltpu.VMEM((d, ff), F32), pltpu.VMEM((ff, d), F32),
                        pltpu.VMEM((d, 2 * ff), BF16), pltpu.VMEM((ff, d), BF16),
                        pltpu.SemaphoreType.DMA((3,))],
    )
    return pl.pallas_call(
        functools.partial(_gmm_kernel, layer=layer, bm=bm),
        grid_spec=grid_spec,
        out_shape=jax.ShapeDtypeStruct((rows, d // 2), jnp.int32),
        compiler_params=_cparams(("arbitrary",)),
        name="gmm",
    )(blk_e, nxt_e, nused, xs, w_gate, w_up, w_down)


def _combine_kernel(pos_ref, pos_next_ref, os_hbm, gate_ref, x_ref, xb_ref, wgu_ref, wd_ref, g_ref, b_ref,
                    xo_ref, xbo_ref, buf_a, buf_b, y_ref, sems, *, alpha):
    i = pl.program_id(0)
    n_i = pl.num_programs(0)
    tm = buf_a.shape[1]
    ff = wd_ref.shape[0]
    half = buf_a.shape[2]
    bufs = (buf_a, buf_b)

    def row_copy(p, s, k, t):
        return pltpu.make_async_copy(os_hbm.at[pl.ds(p, 1), :], bufs[s].at[k, pl.ds(t, 1), :], sems.at[s])

    def issue(p_ref, col0, s):
        for t in range(tm):
            for k in range(TOP_K):
                row_copy(p_ref[k, col0 + t], s, k, t).start()

    def wait_slot(s):
        for k in range(TOP_K):
            pltpu.make_async_copy(os_hbm.at[pl.ds(0, tm), :], bufs[s].at[k], sems.at[s]).wait()

    def finish(r0, s):
        rows = slice(r0, r0 + tm)
        wait_slot(s)
        lo = jnp.zeros((tm, half), F32)
        hi = jnp.zeros((tm, half), F32)
        for k in range(TOP_K):
            u = bufs[s][k]
            w = gate_ref[rows, k:k + 1]
            lo = lo + lax.bitcast_convert_type(lax.shift_left(u, 16), F32) * w
            hi = hi + lax.bitcast_convert_type(jnp.bitwise_and(u, -65536), F32) * w
        routed = jnp.concatenate([lo, hi], axis=1)
        z = _layer_norm(alpha * x_ref[rows, :] + (routed + y_ref[rows, :]), g_ref[...], b_ref[...])
        xo_ref[rows, :] = z
        xbo_ref[rows, :] = z.astype(BF16)

    @pl.when(i == 0)
    def _():
        def body(t, carry):
            for k in range(TOP_K):
                row_copy(pos_ref[k, t], 0, k, t).start()
            return carry
        lax.fori_loop(0, tm, body, 0)

    issue(pos_ref, tm, 1)
    gu = jnp.dot(xb_ref[...], wgu_ref[...], preferred_element_type=F32)
    gate = gu[:, :ff]
    hmid = gate * jax.nn.sigmoid(gate) * gu[:, ff:]
    y_ref[...] = jnp.dot(hmid.astype(BF16), wd_ref[...], preferred_element_type=F32)
    finish(0, 0)
    issue(pos_next_ref, 0, 0)
    finish(tm, 1)

    @pl.when(i == n_i - 1)
    def _():
        wait_slot(0)


def _combine(os_, pos, gate_t, x, xb, sh_wgu, sh_wd, g, b, *, alpha, tm):
    t, d = x.shape
    tm = _tile(t // 2, tm)
    ff = sh_wd.shape[0]
    n_i = t // (2 * tm)
    row_spec = pl.BlockSpec((2 * tm, d), lambda i: (i, 0))
    vec_spec = pl.BlockSpec((1, d), lambda i: (0, 0))
    return pl.pallas_call(
        functools.partial(_combine_kernel, alpha=alpha),
        grid=(n_i,),
        in_specs=[pl.BlockSpec((TOP_K, 2 * tm), lambda i: (0, i), memory_space=pltpu.SMEM),
                  pl.BlockSpec((TOP_K, 2 * tm), lambda i: (0, jnp.minimum(i + 1, n_i - 1)), memory_space=pltpu.SMEM),
                  pl.BlockSpec(memory_space=pl.ANY),
                  pl.BlockSpec((2 * tm, TOP_K), lambda i: (i, 0)),
                  row_spec, row_spec,
                  pl.BlockSpec((d, 2 * ff), lambda i: (0, 0)),
                  pl.BlockSpec((ff, d), lambda i: (0, 0)),
                  vec_spec, vec_spec],
        out_specs=[row_spec, row_spec],
        out_shape=[jax.ShapeDtypeStruct((t, d), F32), jax.ShapeDtypeStruct((t, d), BF16)],
        scratch_shapes=[pltpu.VMEM((TOP_K, tm, d // 2), jnp.int32), pltpu.VMEM((TOP_K, tm, d // 2), jnp.int32),
                        pltpu.VMEM((2 * tm, d), F32), pltpu.SemaphoreType.DMA((2,))],
        compiler_params=_cparams(("arbitrary",)),
        name="combine",
    )(pos, pos, os_, gate_t, x, xb, sh_wgu, sh_wd, g.reshape(1, d).astype(F32), b.reshape(1, d).astype(F32))


def _moe(x, xb, xp, router_w, router_b, w_gate, w_up, w_down, layer, sh_wg, sh_wu, sh_wd, g, b, *, alpha):
    t, d = x.shape
    bm = GMM_BLOCK
    eidx, gate, rank, counts = _router(x, router_w, router_b, tm=512)
    counts = counts.reshape(N_EXPERTS)
    padded = (counts + bm - 1) // bm * bm
    pad_end = jnp.cumsum(padded)
    pad_start = pad_end - padded
    n_blocks = t * TOP_K // bm + N_EXPERTS
    nused = (pad_end[-1] // bm).astype(jnp.int32).reshape(1)
    blk_start = jnp.minimum(jnp.arange(n_blocks, dtype=jnp.int32), nused - 1) * bm
    blk_e = jnp.sum((pad_end[None, :] <= blk_start[:, None]).astype(jnp.int32), axis=1)
    blk_e = jnp.minimum(blk_e, N_EXPERTS - 1).astype(jnp.int32)
    e_ids = jnp.arange(N_EXPERTS, dtype=jnp.int32)
    later_used = jnp.logical_and(e_ids[None, :] > e_ids[:, None], counts[None, :] > 0)
    next_used = jnp.min(jnp.where(later_used, e_ids[None, :], N_EXPERTS), axis=1)
    next_used = jnp.where(next_used == N_EXPERTS, -1, next_used).astype(jnp.int32)
    nxt_e = jnp.sum(jnp.where(blk_e[:, None] == e_ids[None, :], next_used[None, :], 0), axis=1).astype(jnp.int32)
    start_of = jnp.sum(jnp.where(eidx[:, :, None] == e_ids, pad_start, 0), axis=-1)
    pos = (start_of + rank).astype(jnp.int32)
    xs = _dispatch(xp, pos, n_blocks * bm, tm=256)
    os_ = _gmm(xs, w_gate, w_up, w_down, layer, blk_e, nxt_e, nused, bm=bm)
    sh_wgu = jnp.concatenate([sh_wg, sh_wu], axis=1).astype(BF16)
    return _combine(os_, pos, gate.T, x, xb, sh_wgu, sh_wd.astype(BF16), g, b, alpha=alpha, tm=128)


def kernel(x, positions, even_w_in, even_w_out, a_lambda, a_subln_g, b_conv_w, odd_w_in, odd_w_out, c_sinks,
           ln1_g, ln1_b, ln2_g, ln2_b, router_w, router_b, exp_w_gate, exp_w_up, exp_w_down,
           sh_w_gate, sh_w_up, sh_w_down):
    bn, s_len, d = x.shape
    depth = ln1_g.shape[0]
    alpha = (2.0 * depth) ** 0.25
    outs = []
    for bi in range(bn):
        xf = x[bi]
        xb = xf
        tab_a = _rope_tables(positions[bi], A_QK_DIM, A_ROT_DIM)
        tab_c = _rope_tables(positions[bi], C_HEAD_DIM, C_ROT_DIM)
        for layer in range(depth):
            j = layer // 2
            if layer % 2 == 0:
                h = _proj_in(xb, even_w_in, j, tab_a, q_width=A_QK_WIDTH, k_width=A_QK_WIDTH,
                             half=A_ROT_DIM // 2, q_scale=A_QK_DIM ** -0.5 * LOG2E, tm=1024, tn=512)
                lam_init = 0.8 - 0.6 * math.exp(-0.3 * layer)
                lv = a_lambda[j].astype(F32)
                lam = (jnp.exp(jnp.sum(lv[0] * lv[1])) - jnp.exp(jnp.sum(lv[2] * lv[3])) + lam_init).reshape(1)
                y_a = _diff_attn(h, lam, a_subln_g[j], lam_init=lam_init, tq=512, tk=512)
                y_b = _short_conv(h, b_conv_w[j], tr=1024, tc=512)
                ys = [y_a, y_b]
                w_out = even_w_out[j]
            else:
                h = _proj_in(xb, odd_w_in, j, tab_c, q_width=C_Q_WIDTH, k_width=C_KV_WIDTH,
                             half=C_ROT_DIM // 2, q_scale=C_HEAD_DIM ** -0.5 * LOG2E, tm=1024, tn=512)
                ys = [_win_attn(h, c_sinks[j].astype(F32) * LOG2E, tq=256)]
                w_out = odd_w_out[j]
            xf, xb, xp = _proj_out_ln(ys, w_out.astype(BF16), xf, ln1_g[layer], ln1_b[layer], alpha=alpha, tm=256)
            xf, xb = _moe(xf, xb, xp, router_w[layer], router_b[layer], exp_w_gate, exp_w_up, exp_w_down, layer,
                          sh_w_gate[layer], sh_w_up[layer], sh_w_down[layer],
                          ln2_g[layer], ln2_b[layer], alpha=alpha)
        outs.append(xf)
    return jnp.stack(outs, axis=0)
```

```python
import functools
import math

import jax
import jax.numpy as jnp
from jax import lax
from jax.experimental import pallas as pl
from jax.experimental.pallas import tpu as pltpu

F32 = jnp.float32
BF16 = jnp.bfloat16

LANES = 128
MXU_CHUNKS = 2
VMEM_LIMIT_BYTES = 56 * 1024 * 1024

A_HEADS = 4
A_QK_DIM = 128
A_V_DIM = 256
A_QK_WIDTH = 1024
A_WIDTH = 1024
B_WIDTH = 1024
C_Q_HEADS = 32
C_KV_HEADS = 4
C_HEAD_DIM = 64
C_GROUP = C_Q_HEADS // C_KV_HEADS
C_Q_WIDTH = C_Q_HEADS * C_HEAD_DIM
C_KV_WIDTH = C_KV_HEADS * C_HEAD_DIM
WINDOW = 128
ROPE_THETA = 500000.0
A_ROT_DIM = A_QK_DIM // 4
C_ROT_DIM = C_HEAD_DIM // 4
N_EXPERTS = 64
TOP_K = 8
N_GROUPS = 8
TOPK_GROUPS = 4
EXPERTS_PER_GROUP = N_EXPERTS // N_GROUPS
ROUTE_SCALE = 2.5
LN_EPS = 1e-5
LOG2E = math.log2(math.e)

GMM_BLOCK = 256
GMM_BLOCKS_PER_STEP = 8
CAST_ROWS = 64


def _cparams(semantics):
    return pltpu.CompilerParams(dimension_semantics=semantics, vmem_limit_bytes=VMEM_LIMIT_BYTES)


def _tile(dim, want):
    t = min(dim, want)
    assert dim % t == 0, (dim, want)
    return t


def _rope_tables(positions, head_dim, rot_dim):
    half = rot_dim // 2
    inv_freq = ROPE_THETA ** (-jnp.arange(half, dtype=F32) * 2.0 / rot_dim)
    ang = positions.astype(F32)[:, None] * inv_freq
    cos, sin = jnp.cos(ang), jnp.sin(ang)
    lane = jnp.arange(LANES) % head_dim
    idx = lane % half
    cos_l = jnp.take(cos, idx, axis=1)
    sin_l = jnp.take(sin, idx, axis=1)
    c = jnp.where(lane < 2 * half, cos_l, 1.0)
    s1 = jnp.where(lane < half, -sin_l, 0.0)
    s2 = jnp.where((lane >= half) & (lane < 2 * half), sin_l, 0.0)
    return c.astype(F32), s1.astype(F32), s2.astype(F32)


def _proj_in_kernel(x_ref, w_ref, c_ref, s1_ref, s2_ref, o_ref, wb_ref, *, tile_kinds, half, q_scale):
    j = pl.program_id(0)

    @pl.when(pl.program_id(1) == 0)
    def _():
        wb_ref[...] = w_ref[...].astype(BF16)

    n_chunks = o_ref.shape[1] // LANES
    groups = {}
    for t, kinds in enumerate(tile_kinds):
        groups.setdefault(kinds, []).append(t)
    for kinds, tiles in groups.items():
        cond = functools.reduce(jnp.logical_or, [j == t for t in tiles])

        @pl.when(cond)
        def _(kinds=kinds):
            x = x_ref[...].astype(BF16)
            for c in range(n_chunks):
                if c % MXU_CHUNKS == 0:
                    wide = jnp.dot(x, wb_ref[:, c * LANES:(c + MXU_CHUNKS) * LANES], preferred_element_type=F32)
                r = wide[:, (c % MXU_CHUNKS) * LANES:(c % MXU_CHUNKS + 1) * LANES]
                if kinds[c] != "n":
                    r = (r * c_ref[...] + pltpu.roll(r, LANES - half, 1) * s1_ref[...]
                         + pltpu.roll(r, half, 1) * s2_ref[...])
                    if kinds[c] == "q":
                        r = r * q_scale
                o_ref[:, c * LANES:(c + 1) * LANES] = r.astype(o_ref.dtype)


def _proj_in(x, w_stack, w_idx, tables, *, q_width, k_width, half, q_scale, tm, tn):
    m, kdim = x.shape
    n = w_stack.shape[2]
    tm = _tile(m, tm)
    tn = _tile(n, tn)

    def kind(col):
        return "q" if col < q_width else ("k" if col < q_width + k_width else "n")

    tile_kinds = tuple(tuple(kind(t * tn + c * LANES) for c in range(tn // LANES)) for t in range(n // tn))
    c, s1, s2 = tables
    tab_spec = pl.BlockSpec((tm, LANES), lambda j, i: (i, 0))
    return pl.pallas_call(
        functools.partial(_proj_in_kernel, tile_kinds=tile_kinds, half=half, q_scale=q_scale),
        grid=(n // tn, m // tm),
        in_specs=[pl.BlockSpec((tm, kdim), lambda j, i: (i, 0)),
                  pl.BlockSpec((None, kdim, tn), lambda j, i: (w_idx, 0, j)),
                  tab_spec, tab_spec, tab_spec],
        out_specs=pl.BlockSpec((tm, tn), lambda j, i: (i, j)),
        out_shape=jax.ShapeDtypeStruct((m, n), BF16),
        scratch_shapes=[pltpu.VMEM((kdim, tn), BF16)],
        compiler_params=_cparams(("arbitrary", "arbitrary")),
        name="proj_in",
    )(x, w_stack, c, s1, s2)


def _diff_attn_kernel(lam_ref, q_ref, k_ref, vt_ref, g_ref, o_ref, m_ref, l_ref, acc_ref, s_ref, *, tk, lam_init):
    n_kt = vt_ref.shape[0]
    dk = A_QK_DIM
    nt = (((1,), (1,)), ((), ()))
    m_ref[...] = jnp.full(m_ref.shape, -jnp.inf, F32)
    l_ref[...] = jnp.zeros(l_ref.shape, F32)
    acc_ref[...] = jnp.zeros(acc_ref.shape, F32)

    def scores(kk, slot):
        ks = pl.multiple_of(kk * tk, tk)
        for mi in range(2):
            kb = k_ref[pl.ds(ks, tk), mi * dk:(mi + 1) * dk]
            qm = q_ref[:, mi * dk:(mi + 1) * dk]
            s_ref[slot, mi] = lax.dot_general(kb, qm, nt, preferred_element_type=F32)

    def softmax_pv(kk, slot):
        vt = vt_ref[kk]
        for mi in range(2):
            st = s_ref[slot, mi]
            m_old = m_ref[mi]
            m_new = jnp.maximum(m_old, jnp.max(st, axis=0, keepdims=True))
            alpha = jnp.exp2(m_old - m_new)
            pt = jnp.exp2(st - m_new)
            l_ref[mi] = alpha * l_ref[mi] + jnp.sum(pt, axis=0, keepdims=True)
            acc_ref[mi] = alpha * acc_ref[mi] + jnp.dot(vt, pt.astype(BF16), preferred_element_type=F32)
            m_ref[mi] = m_new

    scores(0, 0)

    def body(kk2, carry):
        kk = kk2 * 2
        scores(kk + 1, 1)
        softmax_pv(kk, 0)
        scores(jnp.minimum(kk + 2, n_kt - 1), 0)
        softmax_pv(kk + 1, 1)
        return carry

    lax.fori_loop(0, n_kt // 2, body, 0)
    lam = lam_ref[0]
    o = acc_ref[0] / l_ref[0] - lam * (acc_ref[1] / l_ref[1])
    ms = jnp.mean(o * o, axis=0, keepdims=True)
    o = o * lax.rsqrt(ms + LN_EPS) * g_ref[...] * (1.0 - lam_init)
    o_ref[...] = o.T.astype(o_ref.dtype)


def _diff_attn(h, lam, subln_g, *, lam_init, tq, tk):
    s_len = h.shape[0]
    tq = _tile(s_len, tq)
    tk = _tile(s_len, tk)
    w = 2 * A_QK_DIM
    k_off = A_QK_WIDTH // w
    n_kt = s_len // tk
    vt = h[:, 2 * A_QK_WIDTH:2 * A_QK_WIDTH + A_WIDTH].reshape(n_kt, tk, A_HEADS, A_V_DIM).transpose(2, 0, 3, 1)
    return pl.pallas_call(
        functools.partial(_diff_attn_kernel, tk=tk, lam_init=lam_init),
        grid=(A_HEADS, s_len // tq),
        in_specs=[pl.BlockSpec(memory_space=pltpu.SMEM),
                  pl.BlockSpec((tq, w), lambda hh, i: (i, hh)),
                  pl.BlockSpec((s_len, w), lambda hh, i: (0, k_off + hh)),
                  pl.BlockSpec((None, n_kt, A_V_DIM, tk), lambda hh, i: (hh, 0, 0, 0)),
                  pl.BlockSpec((A_V_DIM, 1), lambda hh, i: (0, 0))],
        out_specs=pl.BlockSpec((tq, A_V_DIM), lambda hh, i: (i, hh)),
        out_shape=jax.ShapeDtypeStruct((s_len, A_WIDTH), BF16),
        scratch_shapes=[pltpu.VMEM((2, 1, tq), F32), pltpu.VMEM((2, 1, tq), F32),
                        pltpu.VMEM((2, A_V_DIM, tq), F32), pltpu.VMEM((2, 2, tk, tq), F32)],
        compiler_params=_cparams(("arbitrary", "arbitrary")),
        name="diff_attn",
    )(lam, h, h, vt, subln_g.reshape(A_V_DIM, 1).astype(F32))


HALO = 16


def _short_conv_kernel(bg_ref, cg_ref, hb_ref, cgp_ref, hbp_ref, cgn_ref, hbn_ref, w_ref, o_ref):
    i = pl.program_id(0)
    n_i = pl.num_programs(0)
    u = cg_ref[...].astype(F32) * hb_ref[...].astype(F32)
    tr = u.shape[0]
    prev_row = cgp_ref[HALO - 1:HALO, :].astype(F32) * hbp_ref[HALO - 1:HALO, :].astype(F32)
    next_row = cgn_ref[0:1, :].astype(F32) * hbn_ref[0:1, :].astype(F32)
    prev_row = jnp.where(i == 0, 0.0, prev_row)
    next_row = jnp.where(i == n_i - 1, 0.0, next_row)
    row = lax.broadcasted_iota(jnp.int32, u.shape, 0)
    u_prev = jnp.where(row == 0, prev_row, pltpu.roll(u, 1, 0))
    u_next = jnp.where(row == tr - 1, next_row, pltpu.roll(u, tr - 1, 0))
    w = w_ref[...]
    y = w[0:1, :] * u_prev + w[1:2, :] * u + w[2:3, :] * u_next
    o_ref[...] = (bg_ref[...].astype(F32) * y).astype(o_ref.dtype)


def _short_conv(h, conv_w, *, tr, tc):
    s_len = h.shape[0]
    tr = _tile(s_len, tr)
    tc = _tile(B_WIDTH, tc)
    base = 2 * A_QK_WIDTH + A_WIDTH
    bg_off, cg_off, hb_off = base // tc, (base + B_WIDTH) // tc, (base + 2 * B_WIDTH) // tc
    rb = tr // HALO
    last = s_len // HALO - 1

    def main(off):
        return pl.BlockSpec((tr, tc), lambda i, c: (i, off + c))

    def prev(off):
        return pl.BlockSpec((HALO, tc), lambda i, c: (jnp.maximum(i * rb - 1, 0), off + c))

    def nxt(off):
        return pl.BlockSpec((HALO, tc), lambda i, c: (jnp.minimum((i + 1) * rb, last), off + c))

    return pl.pallas_call(
        _short_conv_kernel,
        grid=(s_len // tr, B_WIDTH // tc),
        in_specs=[main(bg_off), main(cg_off), main(hb_off), prev(cg_off), prev(hb_off), nxt(cg_off), nxt(hb_off),
                  pl.BlockSpec((3, tc), lambda i, c: (0, c))],
        out_specs=pl.BlockSpec((tr, tc), lambda i, c: (i, c)),
        out_shape=jax.ShapeDtypeStruct((s_len, B_WIDTH), BF16),
        compiler_params=_cparams(("arbitrary", "arbitrary")),
        name="short_conv",
    )(h, h, h, h, h, h, h, conv_w.astype(F32))


KEY_TILE = 128


def _win_first_tile(i, tq, n_win, n_kt):
    return jnp.clip(i * (tq // KEY_TILE) - WINDOW // KEY_TILE, 0, n_kt - n_win)


def _win_attn_kernel(sink_ref, q_ref, *refs, tq, n_win, n_kt):
    klo_refs = refs[:n_win]
    khi_refs = refs[n_win:2 * n_win]
    vt_refs = refs[2 * n_win:3 * n_win]
    o_ref, s_ref, ot_ref = refs[3 * n_win:]
    i = pl.program_id(0)
    win = n_win * KEY_TILE
    hd = C_HEAD_DIM
    nt = (((1,), (1,)), ((), ()))
    q_start = i * tq
    k_start = _win_first_tile(i, tq, n_win, n_kt) * KEY_TILE
    kpos = k_start + lax.broadcasted_iota(jnp.int32, (win, tq), 0)
    qpos = q_start + lax.broadcasted_iota(jnp.int32, (win, tq), 1)
    bias = jnp.where(jnp.abs(qpos - kpos) <= WINDOW, 0.0, -jnp.inf)
    k_lo = jnp.concatenate([r[...] for r in klo_refs], axis=0)
    k_hi = jnp.concatenate([r[...] for r in khi_refs], axis=0)
    vt = jnp.concatenate([r[...] for r in vt_refs], axis=1)
    k_sel = (k_lo, k_hi)

    def scores(qh):
        pair, half = divmod(qh, 2)
        g = qh // C_GROUP
        qp = q_ref[:, pair * LANES:(pair + 1) * LANES]
        st = lax.dot_general(k_sel[half][:, g * LANES:(g + 1) * LANES], qp, nt, preferred_element_type=F32)
        s_ref[qh % 2] = st + bias

    def softmax_pv(qh):
        pair, half = divmod(qh, 2)
        g = qh // C_GROUP
        sink = sink_ref[qh]
        st = s_ref[qh % 2]
        m = jnp.maximum(jnp.max(st, axis=0, keepdims=True), sink)
        e = jnp.exp2(st - m)
        denom = jnp.sum(e, axis=0, keepdims=True) + jnp.exp2(sink - m)
        ot_ref[half * hd:(half + 1) * hd, :] = jnp.dot(vt[g * hd:(g + 1) * hd, :], e.astype(BF16),
                                                       preferred_element_type=F32) / denom
        if half == 1:
            o_ref[:, pair * LANES:(pair + 1) * LANES] = ot_ref[...].T.astype(o_ref.dtype)

    scores(0)
    for qh in range(C_Q_HEADS):
        if qh + 1 < C_Q_HEADS:
            scores(qh + 1)
        softmax_pv(qh)


def _win_attn(h, sinks_log2, *, tq):
    s_len = h.shape[0]
    tq = _tile(s_len, tq)
    n_win = (tq + 2 * WINDOW) // KEY_TILE
    n_kt = s_len // KEY_TILE
    assert n_kt >= n_win
    hd = C_HEAD_DIM
    k = h[:, C_Q_WIDTH:C_Q_WIDTH + C_KV_WIDTH].reshape(s_len, C_KV_HEADS, hd)
    zeros = jnp.zeros_like(k)
    k_lo = jnp.concatenate([k, zeros], axis=-1).reshape(s_len, C_KV_HEADS * LANES)
    k_hi = jnp.concatenate([zeros, k], axis=-1).reshape(s_len, C_KV_HEADS * LANES)
    vt = h[:, C_Q_WIDTH + C_KV_WIDTH:].reshape(n_kt, KEY_TILE, C_KV_WIDTH).transpose(0, 2, 1)

    def k_spec(j):
        return pl.BlockSpec((KEY_TILE, C_KV_HEADS * LANES),
                            lambda i: (_win_first_tile(i, tq, n_win, n_kt) + j, 0))

    def vt_spec(j):
        return pl.BlockSpec((None, C_KV_WIDTH, KEY_TILE),
                            lambda i: (_win_first_tile(i, tq, n_win, n_kt) + j, 0, 0))

    return pl.pallas_call(
        functools.partial(_win_attn_kernel, tq=tq, n_win=n_win, n_kt=n_kt),
        grid=(s_len // tq,),
        in_specs=[pl.BlockSpec(memory_space=pltpu.SMEM),
                  pl.BlockSpec((tq, C_Q_WIDTH), lambda i: (i, 0))]
                 + [k_spec(j) for j in range(n_win)] + [k_spec(j) for j in range(n_win)]
                 + [vt_spec(j) for j in range(n_win)],
        out_specs=pl.BlockSpec((tq, C_Q_WIDTH), lambda i: (i, 0)),
        out_shape=jax.ShapeDtypeStruct((s_len, C_Q_WIDTH), BF16),
        scratch_shapes=[pltpu.VMEM((2, n_win * KEY_TILE, tq), F32), pltpu.VMEM((2 * hd, tq), F32)],
        compiler_params=_cparams(("arbitrary",)),
        name="win_attn",
    )(sinks_log2, h, *([k_lo] * n_win), *([k_hi] * n_win), *([vt] * n_win))


def _layer_norm(z, g, b):
    mu = jnp.mean(z, axis=-1, keepdims=True)
    zc = z - mu
    var = jnp.mean(zc * zc, axis=-1, keepdims=True)
    return zc * lax.rsqrt(var + LN_EPS) * g + b


def _pack_bf16_pairs(z):
    n = z.shape[1] // 2
    lo = lax.bitcast_convert_type(z[:, :n].astype(jnp.bfloat16).astype(F32), jnp.int32)
    hi = lax.bitcast_convert_type(z[:, n:].astype(jnp.bfloat16).astype(F32), jnp.int32)
    return jnp.bitwise_or(hi, lax.shift_right_logical(lo, 16))


def _unpack_bf16_pairs(u):
    lo = lax.bitcast_convert_type(lax.shift_left(u, 16), F32).astype(BF16)
    hi = lax.bitcast_convert_type(jnp.bitwise_and(u, -65536), F32).astype(BF16)
    return jnp.concatenate([lo, hi], axis=1)


def _proj_out_ln_kernel(*refs, n_in, alpha):
    y_refs = refs[:n_in]
    w_ref, x_ref, g_ref, b_ref, xo_ref, xb_ref, xp_ref = refs[n_in:]
    acc = None
    row = 0
    for y_ref in y_refs:
        kw = y_ref.shape[1]
        part = jnp.dot(y_ref[...], w_ref[row:row + kw, :], preferred_element_type=F32)
        acc = part if acc is None else acc + part
        row += kw
    z = _layer_norm(alpha * x_ref[...] + acc, g_ref[...], b_ref[...])
    xo_ref[...] = z
    xb_ref[...] = z.astype(BF16)
    xp_ref[...] = _pack_bf16_pairs(z)


def _proj_out_ln(ys, w, x, g, b, *, alpha, tm):
    m, d = x.shape
    tm = _tile(m, tm)
    kdim = w.shape[0]
    in_specs = [pl.BlockSpec((tm, y.shape[1]), lambda i: (i, 0)) for y in ys]
    in_specs += [pl.BlockSpec((kdim, d), lambda i: (0, 0)),
                 pl.BlockSpec((tm, d), lambda i: (i, 0)),
                 pl.BlockSpec((1, d), lambda i: (0, 0)),
                 pl.BlockSpec((1, d), lambda i: (0, 0))]
    return pl.pallas_call(
        functools.partial(_proj_out_ln_kernel, n_in=len(ys), alpha=alpha),
        grid=(m // tm,),
        in_specs=in_specs,
        out_specs=[pl.BlockSpec((tm, d), lambda i: (i, 0)), pl.BlockSpec((tm, d), lambda i: (i, 0)),
                   pl.BlockSpec((tm, d // 2), lambda i: (i, 0))],
        out_shape=[jax.ShapeDtypeStruct((m, d), F32), jax.ShapeDtypeStruct((m, d), BF16),
                   jax.ShapeDtypeStruct((m, d // 2), jnp.int32)],
        compiler_params=_cparams(("arbitrary",)),
        name="proj_out_ln",
    )(*ys, w, x, g.reshape(1, d).astype(F32), b.reshape(1, d).astype(F32))


def _first_argmax(vals, iota, size):
    mx = jnp.max(vals, axis=0, keepdims=True)
    idx = jnp.min(jnp.where(vals == mx, iota, size), axis=0, keepdims=True)
    return mx, idx


def _router_kernel(x_ref, wt_ref, b_ref, eidx_ref, gate_ref, rank_ref, cnt_ref, carry_ref):
    i = pl.program_id(0)
    tm = x_ref.shape[0]
    neg = -jnp.inf

    @pl.when(i == 0)
    def _():
        carry_ref[...] = jnp.zeros(carry_ref.shape, F32)

    logits = lax.dot_general(wt_ref[...], x_ref[...], (((1,), (1,)), ((), ())),
                             precision=lax.Precision.HIGHEST, preferred_element_type=F32)
    scores = jax.nn.sigmoid(logits)
    choice = scores + b_ref[...]
    iota_m = lax.broadcasted_iota(jnp.int32, (EXPERTS_PER_GROUP, tm), 0)
    gs_rows = []
    for g in range(N_GROUPS):
        cg = choice[g * EXPERTS_PER_GROUP:(g + 1) * EXPERTS_PER_GROUP, :]
        m1, i1 = _first_argmax(cg, iota_m, EXPERTS_PER_GROUP)
        m2 = jnp.max(jnp.where(iota_m == i1, neg, cg), axis=0, keepdims=True)
        gs_rows.append(m1 + m2)
    gs = jnp.concatenate(gs_rows, axis=0)
    iota_g = lax.broadcasted_iota(jnp.int32, (N_GROUPS, tm), 0)
    sel = jnp.zeros((N_GROUPS, tm), F32)
    for _ in range(TOPK_GROUPS):
        _, gi = _first_argmax(gs, iota_g, N_GROUPS)
        hit = iota_g == gi
        sel = jnp.where(hit, 1.0, sel)
        gs = jnp.where(hit, neg, gs)
    allowed = jnp.concatenate(
        [jnp.broadcast_to(sel[g:g + 1, :], (EXPERTS_PER_GROUP, tm)) for g in range(N_GROUPS)], axis=0)
    iota_e = lax.broadcasted_iota(jnp.int32, (N_EXPERTS, tm), 0)
    masked = jnp.where(allowed > 0.0, choice, neg)
    onehot = jnp.zeros((N_EXPERTS, tm), F32)
    e_rows, w_rows = [], []
    for _ in range(TOP_K):
        _, ei = _first_argmax(masked, iota_e, N_EXPERTS)
        hit = iota_e == ei
        e_rows.append(ei)
        w_rows.append(jnp.sum(jnp.where(hit, scores, 0.0), axis=0, keepdims=True))
        onehot = jnp.where(hit, 1.0, onehot)
        masked = jnp.where(hit, neg, masked)
    wsel = jnp.concatenate(w_rows, axis=0)
    gate_ref[...] = wsel / jnp.sum(wsel, axis=0, keepdims=True) * ROUTE_SCALE
    eidx_ref[...] = jnp.concatenate(e_rows, axis=0)
    tri = (lax.broadcasted_iota(jnp.int32, (tm, tm), 0) < lax.broadcasted_iota(jnp.int32, (tm, tm), 1))
    prefix = jnp.dot(onehot.astype(BF16), tri.astype(BF16), preferred_element_type=F32)
    base = prefix + carry_ref[...]
    r_rows = [jnp.sum(jnp.where(iota_e == ei, base, 0.0), axis=0, keepdims=True) for ei in e_rows]
    rank_ref[...] = jnp.concatenate(r_rows, axis=0).astype(jnp.int32)
    carry_ref[...] = carry_ref[...] + jnp.sum(onehot, axis=1, keepdims=True)
    cnt_ref[...] = carry_ref[...].astype(jnp.int32)


def _router(x, router_w, router_b, *, tm):
    t, d = x.shape
    tm = _tile(t, tm)
    wt = router_w.astype(F32).T
    kt_spec = pl.BlockSpec((TOP_K, tm), lambda i: (0, i))
    return pl.pallas_call(
        _router_kernel,
        grid=(t // tm,),
        in_specs=[pl.BlockSpec((tm, d), lambda i: (i, 0)),
                  pl.BlockSpec((N_EXPERTS, d), lambda i: (0, 0)),
                  pl.BlockSpec((N_EXPERTS, 1), lambda i: (0, 0))],
        out_specs=[kt_spec, kt_spec, kt_spec, pl.BlockSpec((N_EXPERTS, 1), lambda i: (0, 0))],
        out_shape=[jax.ShapeDtypeStruct((TOP_K, t), jnp.int32), jax.ShapeDtypeStruct((TOP_K, t), F32),
                   jax.ShapeDtypeStruct((TOP_K, t), jnp.int32), jax.ShapeDtypeStruct((N_EXPERTS, 1), jnp.int32)],
        scratch_shapes=[pltpu.VMEM((N_EXPERTS, 1), F32)],
        compiler_params=_cparams(("arbitrary",)),
        name="router",
    )(x, wt, router_b.astype(F32).reshape(N_EXPERTS, 1))


def _dispatch_kernel(pos_ref, x_ref, xs_hbm, sem):
    tm = x_ref.shape[0]
    for t in range(tm):
        for k in range(TOP_K):
            pltpu.make_async_copy(x_ref.at[pl.ds(t, 1), :], xs_hbm.at[pl.ds(pos_ref[k, t], 1), :], sem).start()
    for k in range(TOP_K):
        pltpu.make_async_copy(x_ref, xs_hbm.at[pl.ds(0, tm), :], sem).wait()


def _dispatch(x, pos, rows, *, tm):
    t, d = x.shape
    tm = _tile(t, tm)
    return pl.pallas_call(
        _dispatch_kernel,
        grid=(t // tm,),
        in_specs=[pl.BlockSpec((TOP_K, tm), lambda i: (0, i), memory_space=pltpu.SMEM),
                  pl.BlockSpec((tm, d), lambda i: (i, 0))],
        out_specs=pl.BlockSpec(memory_space=pl.ANY),
        out_shape=jax.ShapeDtypeStruct((rows, d), x.dtype),
        scratch_shapes=[pltpu.SemaphoreType.DMA(())],
        compiler_params=_cparams(("arbitrary",)),
        name="dispatch",
    )(pos, x)


def _expert_weight_copies(w_hbm, stage, sems, layer, e):
    return [pltpu.make_async_copy(w.at[layer, e], st, sems.at[i]) for i, (w, st) in enumerate(zip(w_hbm, stage))]


def _gmm_kernel(blk_e_ref, nxt_e_ref, nused_ref, xs_ref, wg_hbm, wu_hbm, wd_hbm, o_ref,
                stage_g, stage_u, stage_d, wgu_s, wd_s, sems, *, layer, bm):
    step = pl.program_id(0)
    ff = stage_g.shape[1]
    copies = functools.partial(_expert_weight_copies, (wg_hbm, wu_hbm, wd_hbm), (stage_g, stage_u, stage_d), sems,
                               layer)

    @pl.when(step == 0)
    def _():
        for c in copies(blk_e_ref[0]):
            c.start()

    for sub in range(xs_ref.shape[0] // bm):
        b = step * (xs_ref.shape[0] // bm) + sub
        rows = slice(sub * bm, (sub + 1) * bm)
        used = b < nused_ref[0]
        new_e = jnp.logical_or(b == 0, blk_e_ref[b] != blk_e_ref[jnp.maximum(b - 1, 0)])

        @pl.when(jnp.logical_and(used, new_e))
        def _(b=b):
            for c in copies(blk_e_ref[b]):
                c.wait()
            for r in range(0, stage_g.shape[0], CAST_ROWS):
                wgu_s[r:r + CAST_ROWS, :ff] = stage_g[r:r + CAST_ROWS, :].astype(BF16)
                wgu_s[r:r + CAST_ROWS, ff:] = stage_u[r:r + CAST_ROWS, :].astype(BF16)
            for r in range(0, stage_d.shape[0], CAST_ROWS):
                wd_s[r:r + CAST_ROWS, :] = stage_d[r:r + CAST_ROWS, :].astype(BF16)

            @pl.when(nxt_e_ref[b] >= 0)
            def _():
                for c in copies(nxt_e_ref[b]):
                    c.start()

        @pl.when(used)
        def _(rows=rows):
            x = _unpack_bf16_pairs(xs_ref[rows, :])
            gu = jnp.dot(x, wgu_s[...], preferred_element_type=F32)
            gate = gu[:, :ff]
            hmid = gate * jax.nn.sigmoid(gate) * gu[:, ff:]
            o_ref[rows, :] = _pack_bf16_pairs(jnp.dot(hmid.astype(BF16), wd_s[...], preferred_element_type=F32))


def _gmm(xs, w_gate, w_up, w_down, layer, blk_e, nxt_e, nused, *, bm):
    rows = xs.shape[0]
    d, ff = w_gate.shape[2], w_gate.shape[3]
    n_blocks = rows // bm
    per_step = GMM_BLOCKS_PER_STEP
    assert n_blocks % per_step == 0

    def blk(s, be, nx, nu):
        return (jnp.minimum(s, (nu[0] - 1) // per_step), 0)

    grid_spec = pltpu.PrefetchScalarGridSpec(
        num_scalar_prefetch=3,
        grid=(n_blocks // per_step,),
        in_specs=[pl.BlockSpec((per_step * bm, d // 2), blk),
                  pl.BlockSpec(memory_space=pl.ANY), pl.BlockSpec(memory_space=pl.ANY),
                  pl.BlockSpec(memory_space=pl.ANY)],
        out_specs=pl.BlockSpec((per_step * bm, d // 2), blk),
        scratch_shapes=[pltpu.VMEM((d, ff), F32), pltpu.VMEM((d, ff), F32), pltpu.VMEM((ff, d), F32),
                        pltpu.VMEM((d, 2 * ff), BF16), pltpu.VMEM((ff, d), BF16),
                        pltpu.SemaphoreType.DMA((3,))],
    )
    return pl.pallas_call(
        functools.partial(_gmm_kernel, layer=layer, bm=bm),
        grid_spec=grid_spec,
        out_shape=jax.ShapeDtypeStruct((rows, d // 2), jnp.int32),
        compiler_params=_cparams(("arbitrary",)),
        name="gmm",
    )(blk_e, nxt_e, nused, xs, w_gate, w_up, w_down)


def _combine_kernel(pos_ref, pos_next_ref, os_hbm, gate_ref, x_ref, xb_ref, wgu_ref, wd_ref, g_ref, b_ref,
                    xo_ref, xbo_ref, buf_a, buf_b, y_ref, sems, *, alpha):
    i = pl.program_id(0)
    n_i = pl.num_programs(0)
    tm = buf_a.shape[1]
    ff = wd_ref.shape[0]
    half = buf_a.shape[2]
    bufs = (buf_a, buf_b)

    def row_copy(p, s, k, t):
        return pltpu.make_async_copy(os_hbm.at[pl.ds(p, 1), :], bufs[s].at[k, pl.ds(t, 1), :], sems.at[s])

    def issue(p_ref, col0, s):
        for t in range(tm):
            for k in range(TOP_K):
                row_copy(p_ref[k, col0 + t], s, k, t).start()

    def wait_slot(s):
        for k in range(TOP_K):
            pltpu.make_async_copy(os_hbm.at[pl.ds(0, tm), :], bufs[s].at[k], sems.at[s]).wait()

    def finish(r0, s):
        rows = slice(r0, r0 + tm)
        wait_slot(s)
        lo = jnp.zeros((tm, half), F32)
        hi = jnp.zeros((tm, half), F32)
        for k in range(TOP_K):
            u = bufs[s][k]
            w = gate_ref[rows, k:k + 1]
            lo = lo + lax.bitcast_convert_type(lax.shift_left(u, 16), F32) * w
            hi = hi + lax.bitcast_convert_type(jnp.bitwise_and(u, -65536), F32) * w
        routed = jnp.concatenate([lo, hi], axis=1)
        z = _layer_norm(alpha * x_ref[rows, :] + (routed + y_ref[rows, :]), g_ref[...], b_ref[...])
        xo_ref[rows, :] = z
        xbo_ref[rows, :] = z.astype(BF16)

    @pl.when(i == 0)
    def _():
        def body(t, carry):
            for k in range(TOP_K):
                row_copy(pos_ref[k, t], 0, k, t).start()
            return carry
        lax.fori_loop(0, tm, body, 0)

    issue(pos_ref, tm, 1)
    gu = jnp.dot(xb_ref[...], wgu_ref[...], preferred_element_type=F32)
    gate = gu[:, :ff]
    hmid = gate * jax.nn.sigmoid(gate) * gu[:, ff:]
    y_ref[...] = jnp.dot(hmid.astype(BF16), wd_ref[...], preferred_element_type=F32)
    finish(0, 0)
    issue(pos_next_ref, 0, 0)
    finish(tm, 1)

    @pl.when(i == n_i - 1)
    def _():
        wait_slot(0)


def _combine(os_, pos, gate_t, x, xb, sh_wgu, sh_wd, g, b, *, alpha, tm):
    t, d = x.shape
    tm = _tile(t // 2, tm)
    ff = sh_wd.shape[0]
    n_i = t // (2 * tm)
    row_spec = pl.BlockSpec((2 * tm, d), lambda i: (i, 0))
    vec_spec = pl.BlockSpec((1, d), lambda i: (0, 0))
    return pl.pallas_call(
        functools.partial(_combine_kernel, alpha=alpha),
        grid=(n_i,),
        in_specs=[pl.BlockSpec((TOP_K, 2 * tm), lambda i: (0, i), memory_space=pltpu.SMEM),
                  pl.BlockSpec((TOP_K, 2 * tm), lambda i: (0, jnp.minimum(i + 1, n_i - 1)), memory_space=pltpu.SMEM),
                  pl.BlockSpec(memory_space=pl.ANY),
                  pl.BlockSpec((2 * tm, TOP_K), lambda i: (i, 0)),
                  row_spec, row_spec,
                  pl.BlockSpec((d, 2 * ff), lambda i: (0, 0)),
                  pl.BlockSpec((ff, d), lambda i: (0, 0)),
                  vec_spec, vec_spec],
        out_specs=[row_spec, row_spec],
        out_shape=[jax.ShapeDtypeStruct((t, d), F32), jax.ShapeDtypeStruct((t, d), BF16)],
        scratch_shapes=[pltpu.VMEM((TOP_K, tm, d // 2), jnp.int32), pltpu.VMEM((TOP_K, tm, d // 2), jnp.int32),
                        pltpu.VMEM((2 * tm, d), F32), pltpu.SemaphoreType.DMA((2,))],
        compiler_params=_cparams(("arbitrary",)),
        name="combine",
    )(pos, pos, os_, gate_t, x, xb, sh_wgu, sh_wd, g.reshape(1, d).astype(F32), b.reshape(1, d).astype(F32))


def _moe(x, xb, xp, router_w, router_b, w_gate, w_up, w_down, layer, sh_wg, sh_wu, sh_wd, g, b, *, alpha):
    t, d = x.shape
    bm = GMM_BLOCK
    eidx, gate, rank, counts = _router(x, router_w, router_b, tm=512)
    counts = counts.reshape(N_EXPERTS)
    padded = (counts + bm - 1) // bm * bm
    pad_end = jnp.cumsum(padded)
    pad_start = pad_end - padded
    n_blocks = t * TOP_K // bm + N_EXPERTS
    nused = (pad_end[-1] // bm).astype(jnp.int32).reshape(1)
    blk_start = jnp.minimum(jnp.arange(n_blocks, dtype=jnp.int32), nused - 1) * bm
    blk_e = jnp.sum((pad_end[None, :] <= blk_start[:, None]).astype(jnp.int32), axis=1)
    blk_e = jnp.minimum(blk_e, N_EXPERTS - 1).astype(jnp.int32)
    e_ids = jnp.arange(N_EXPERTS, dtype=jnp.int32)
    later_used = jnp.logical_and(e_ids[None, :] > e_ids[:, None], counts[None, :] > 0)
    next_used = jnp.min(jnp.where(later_used, e_ids[None, :], N_EXPERTS), axis=1)
    next_used = jnp.where(next_used == N_EXPERTS, -1, next_used).astype(jnp.int32)
    nxt_e = jnp.sum(jnp.where(blk_e[:, None] == e_ids[None, :], next_used[None, :], 0), axis=1).astype(jnp.int32)
    start_of = jnp.sum(jnp.where(eidx[:, :, None] == e_ids, pad_start, 0), axis=-1)
    pos = (start_of + rank).astype(jnp.int32)
    xs = _dispatch(xp, pos, n_blocks * bm, tm=256)
    os_ = _gmm(xs, w_gate, w_up, w_down, layer, blk_e, nxt_e, nused, bm=bm)
    sh_wgu = jnp.concatenate([sh_wg, sh_wu], axis=1).astype(BF16)
    return _combine(os_, pos, gate.T, x, xb, sh_wgu, sh_wd.astype(BF16), g, b, alpha=alpha, tm=128)


def kernel(x, positions, even_w_in, even_w_out, a_lambda, a_subln_g, b_conv_w, odd_w_in, odd_w_out, c_sinks,
           ln1_g, ln1_b, ln2_g, ln2_b, router_w, router_b, exp_w_gate, exp_w_up, exp_w_down,
           sh_w_gate, sh_w_up, sh_w_down):
    bn, s_len, d = x.shape
    depth = ln1_g.shape[0]
    alpha = (2.0 * depth) ** 0.25
    outs = []
    for bi in range(bn):
        xf = x[bi]
        xb = xf
        tab_a = _rope_tables(positions[bi], A_QK_DIM, A_ROT_DIM)
        tab_c = _rope_tables(positions[bi], C_HEAD_DIM, C_ROT_DIM)
        for layer in range(depth):
            j = layer // 2
            if layer % 2 == 0:
                h = _proj_in(xb, even_w_in, j, tab_a, q_width=A_QK_WIDTH, k_width=A_QK_WIDTH,
                             half=A_ROT_DIM // 2, q_scale=A_QK_DIM ** -0.5 * LOG2E, tm=1024, tn=512)
                lam_init = 0.8 - 0.6 * math.exp(-0.3 * layer)
                lv = a_lambda[j].astype(F32)
                lam = (jnp.exp(jnp.sum(lv[0] * lv[1])) - jnp.exp(jnp.sum(lv[2] * lv[3])) + lam_init).reshape(1)
                y_a = _diff_attn(h, lam, a_subln_g[j], lam_init=lam_init, tq=1024, tk=512)
                y_b = _short_conv(h, b_conv_w[j], tr=1024, tc=512)
                ys = [y_a, y_b]
                w_out = even_w_out[j]
            else:
                h = _proj_in(xb, odd_w_in, j, tab_c, q_width=C_Q_WIDTH, k_width=C_KV_WIDTH,
                             half=C_ROT_DIM // 2, q_scale=C_HEAD_DIM ** -0.5 * LOG2E, tm=1024, tn=512)
                ys = [_win_attn(h, c_sinks[j].astype(F32) * LOG2E, tq=256)]
                w_out = odd_w_out[j]
            xf, xb, xp = _proj_out_ln(ys, w_out.astype(BF16), xf, ln1_g[layer], ln1_b[layer], alpha=alpha, tm=256)
            xf, xb = _moe(xf, xb, xp, router_w[layer], router_b[layer], exp_w_gate, exp_w_up, exp_w_down, layer,
                          sh_w_gate[layer], sh_w_up[layer], sh_w_down[layer],
                          ln2_g[layer], ln2_b[layer], alpha=alpha)
        outs.append(xf)
    return jnp.stack(outs, axis=0)
```

```python
import functools
import math

import jax
import jax.numpy as jnp
from jax import lax
from jax.experimental import pallas as pl
from jax.experimental.pallas import tpu as pltpu

F32 = jnp.float32
BF16 = jnp.bfloat16

LANES = 128
MXU_CHUNKS = 2
VMEM_LIMIT_BYTES = 56 * 1024 * 1024

A_HEADS = 4
A_QK_DIM = 128
A_V_DIM = 256
A_QK_WIDTH = 1024
A_WIDTH = 1024
B_WIDTH = 1024
C_Q_HEADS = 32
C_KV_HEADS = 4
C_HEAD_DIM = 64
C_GROUP = C_Q_HEADS // C_KV_HEADS
C_Q_WIDTH = C_Q_HEADS * C_HEAD_DIM
C_KV_WIDTH = C_KV_HEADS * C_HEAD_DIM
WINDOW = 128
ROPE_THETA = 500000.0
A_ROT_DIM = A_QK_DIM // 4
C_ROT_DIM = C_HEAD_DIM // 4
N_EXPERTS = 64
TOP_K = 8
N_GROUPS = 8
TOPK_GROUPS = 4
EXPERTS_PER_GROUP = N_EXPERTS // N_GROUPS
ROUTE_SCALE = 2.5
LN_EPS = 1e-5
LOG2E = math.log2(math.e)

GMM_BLOCK = 256
GMM_BLOCKS_PER_STEP = 4
CAST_ROWS = 64


def _cparams(semantics):
    return pltpu.CompilerParams(dimension_semantics=semantics, vmem_limit_bytes=VMEM_LIMIT_BYTES)


def _tile(dim, want):
    t = min(dim, want)
    assert dim % t == 0, (dim, want)
    return t


def _rope_tables(positions, head_dim, rot_dim):
    half = rot_dim // 2
    inv_freq = ROPE_THETA ** (-jnp.arange(half, dtype=F32) * 2.0 / rot_dim)
    ang = positions.astype(F32)[:, None] * inv_freq
    cos, sin = jnp.cos(ang), jnp.sin(ang)
    lane = jnp.arange(LANES) % head_dim
    idx = lane % half
    cos_l = jnp.take(cos, idx, axis=1)
    sin_l = jnp.take(sin, idx, axis=1)
    c = jnp.where(lane < 2 * half, cos_l, 1.0)
    s1 = jnp.where(lane < half, -sin_l, 0.0)
    s2 = jnp.where((lane >= half) & (lane < 2 * half), sin_l, 0.0)
    return c.astype(F32), s1.astype(F32), s2.astype(F32)


def _proj_in_kernel(x_ref, w_ref, c_ref, s1_ref, s2_ref, o_ref, wb_ref, *, tile_kinds, half, q_scale):
    j = pl.program_id(0)

    @pl.when(pl.program_id(1) == 0)
    def _():
        wb_ref[...] = w_ref[...].astype(BF16)

    n_chunks = o_ref.shape[1] // LANES
    groups = {}
    for t, kinds in enumerate(tile_kinds):
        groups.setdefault(kinds, []).append(t)
    for kinds, tiles in groups.items():
        cond = functools.reduce(jnp.logical_or, [j == t for t in tiles])

        @pl.when(cond)
        def _(kinds=kinds):
            x = x_ref[...].astype(BF16)
            for c in range(n_chunks):
                if c % MXU_CHUNKS == 0:
                    wide = jnp.dot(x, wb_ref[:, c * LANES:(c + MXU_CHUNKS) * LANES], preferred_element_type=F32)
                r = wide[:, (c % MXU_CHUNKS) * LANES:(c % MXU_CHUNKS + 1) * LANES]
                if kinds[c] != "n":
                    r = (r * c_ref[...] + pltpu.roll(r, LANES - half, 1) * s1_ref[...]
                         + pltpu.roll(r, half, 1) * s2_ref[...])
                    if kinds[c] == "q":
                        r = r * q_scale
                o_ref[:, c * LANES:(c + 1) * LANES] = r.astype(o_ref.dtype)


def _proj_in(x, w_stack, w_idx, tables, *, q_width, k_width, half, q_scale, tm, tn):
    m, kdim = x.shape
    n = w_stack.shape[2]
    tm = _tile(m, tm)
    tn = _tile(n, tn)

    def kind(col):
        return "q" if col < q_width else ("k" if col < q_width + k_width else "n")

    tile_kinds = tuple(tuple(kind(t * tn + c * LANES) for c in range(tn // LANES)) for t in range(n // tn))
    c, s1, s2 = tables
    tab_spec = pl.BlockSpec((tm, LANES), lambda j, i: (i, 0))
    return pl.pallas_call(
        functools.partial(_proj_in_kernel, tile_kinds=tile_kinds, half=half, q_scale=q_scale),
        grid=(n // tn, m // tm),
        in_specs=[pl.BlockSpec((tm, kdim), lambda j, i: (i, 0)),
                  pl.BlockSpec((None, kdim, tn), lambda j, i: (w_idx, 0, j)),
                  tab_spec, tab_spec, tab_spec],
        out_specs=pl.BlockSpec((tm, tn), lambda j, i: (i, j)),
        out_shape=jax.ShapeDtypeStruct((m, n), BF16),
        scratch_shapes=[pltpu.VMEM((kdim, tn), BF16)],
        compiler_params=_cparams(("arbitrary", "arbitrary")),
        name="proj_in",
    )(x, w_stack, c, s1, s2)


def _diff_attn_kernel(lam_ref, q_ref, k_ref, vt_ref, g_ref, o_ref, m_ref, l_ref, acc_ref, s_ref, *, tk, lam_init):
    n_kt = vt_ref.shape[0]
    dk = A_QK_DIM
    nt = (((1,), (1,)), ((), ()))
    m_ref[...] = jnp.full(m_ref.shape, -jnp.inf, F32)
    l_ref[...] = jnp.zeros(l_ref.shape, F32)
    acc_ref[...] = jnp.zeros(acc_ref.shape, F32)

    def scores(kk, slot):
        ks = pl.multiple_of(kk * tk, tk)
        for mi in range(2):
            kb = k_ref[pl.ds(ks, tk), mi * dk:(mi + 1) * dk]
            qm = q_ref[:, mi * dk:(mi + 1) * dk]
            s_ref[slot, mi] = lax.dot_general(kb, qm, nt, preferred_element_type=F32)

    def softmax_pv(kk, slot):
        vt = vt_ref[kk]
        for mi in range(2):
            st = s_ref[slot, mi]
            m_old = m_ref[mi]
            m_new = jnp.maximum(m_old, jnp.max(st, axis=0, keepdims=True))
            alpha = jnp.exp2(m_old - m_new)
            pt = jnp.exp2(st - m_new)
            l_ref[mi] = alpha * l_ref[mi] + jnp.sum(pt, axis=0, keepdims=True)
            acc_ref[mi] = alpha * acc_ref[mi] + jnp.dot(vt, pt.astype(BF16), preferred_element_type=F32)
            m_ref[mi] = m_new

    scores(0, 0)

    def body(kk2, carry):
        kk = kk2 * 2
        scores(kk + 1, 1)
        softmax_pv(kk, 0)
        scores(jnp.minimum(kk + 2, n_kt - 1), 0)
        softmax_pv(kk + 1, 1)
        return carry

    lax.fori_loop(0, n_kt // 2, body, 0)
    lam = lam_ref[0]
    o = acc_ref[0] / l_ref[0] - lam * (acc_ref[1] / l_ref[1])
    ms = jnp.mean(o * o, axis=0, keepdims=True)
    o = o * lax.rsqrt(ms + LN_EPS) * g_ref[...] * (1.0 - lam_init)
    o_ref[...] = o.T.astype(o_ref.dtype)


def _diff_attn(h, lam, subln_g, *, lam_init, tq, tk):
    s_len = h.shape[0]
    tq = _tile(s_len, tq)
    tk = _tile(s_len, tk)
    w = 2 * A_QK_DIM
    k_off = A_QK_WIDTH // w
    n_kt = s_len // tk
    vt = h[:, 2 * A_QK_WIDTH:2 * A_QK_WIDTH + A_WIDTH].reshape(n_kt, tk, A_HEADS, A_V_DIM).transpose(2, 0, 3, 1)
    return pl.pallas_call(
        functools.partial(_diff_attn_kernel, tk=tk, lam_init=lam_init),
        grid=(A_HEADS, s_len // tq),
        in_specs=[pl.BlockSpec(memory_space=pltpu.SMEM),
                  pl.BlockSpec((tq, w), lambda hh, i: (i, hh)),
                  pl.BlockSpec((s_len, w), lambda hh, i: (0, k_off + hh)),
                  pl.BlockSpec((None, n_kt, A_V_DIM, tk), lambda hh, i: (hh, 0, 0, 0)),
                  pl.BlockSpec((A_V_DIM, 1), lambda hh, i: (0, 0))],
        out_specs=pl.BlockSpec((tq, A_V_DIM), lambda hh, i: (i, hh)),
        out_shape=jax.ShapeDtypeStruct((s_len, A_WIDTH), BF16),
        scratch_shapes=[pltpu.VMEM((2, 1, tq), F32), pltpu.VMEM((2, 1, tq), F32),
                        pltpu.VMEM((2, A_V_DIM, tq), F32), pltpu.VMEM((2, 2, tk, tq), F32)],
        compiler_params=_cparams(("arbitrary", "arbitrary")),
        name="diff_attn",
    )(lam, h, h, vt, subln_g.reshape(A_V_DIM, 1).astype(F32))


HALO = 16


def _short_conv_kernel(bg_ref, cg_ref, hb_ref, cgp_ref, hbp_ref, cgn_ref, hbn_ref, w_ref, o_ref):
    i = pl.program_id(0)
    n_i = pl.num_programs(0)
    u = cg_ref[...].astype(F32) * hb_ref[...].astype(F32)
    tr = u.shape[0]
    prev_row = cgp_ref[HALO - 1:HALO, :].astype(F32) * hbp_ref[HALO - 1:HALO, :].astype(F32)
    next_row = cgn_ref[0:1, :].astype(F32) * hbn_ref[0:1, :].astype(F32)
    prev_row = jnp.where(i == 0, 0.0, prev_row)
    next_row = jnp.where(i == n_i - 1, 0.0, next_row)
    row = lax.broadcasted_iota(jnp.int32, u.shape, 0)
    u_prev = jnp.where(row == 0, prev_row, pltpu.roll(u, 1, 0))
    u_next = jnp.where(row == tr - 1, next_row, pltpu.roll(u, tr - 1, 0))
    w = w_ref[...]
    y = w[0:1, :] * u_prev + w[1:2, :] * u + w[2:3, :] * u_next
    o_ref[...] = (bg_ref[...].astype(F32) * y).astype(o_ref.dtype)


def _short_conv(h, conv_w, *, tr, tc):
    s_len = h.shape[0]
    tr = _tile(s_len, tr)
    tc = _tile(B_WIDTH, tc)
    base = 2 * A_QK_WIDTH + A_WIDTH
    bg_off, cg_off, hb_off = base // tc, (base + B_WIDTH) // tc, (base + 2 * B_WIDTH) // tc
    rb = tr // HALO
    last = s_len // HALO - 1

    def main(off):
        return pl.BlockSpec((tr, tc), lambda i, c: (i, off + c))

    def prev(off):
        return pl.BlockSpec((HALO, tc), lambda i, c: (jnp.maximum(i * rb - 1, 0), off + c))

    def nxt(off):
        return pl.BlockSpec((HALO, tc), lambda i, c: (jnp.minimum((i + 1) * rb, last), off + c))

    return pl.pallas_call(
        _short_conv_kernel,
        grid=(s_len // tr, B_WIDTH // tc),
        in_specs=[main(bg_off), main(cg_off), main(hb_off), prev(cg_off), prev(hb_off), nxt(cg_off), nxt(hb_off),
                  pl.BlockSpec((3, tc), lambda i, c: (0, c))],
        out_specs=pl.BlockSpec((tr, tc), lambda i, c: (i, c)),
        out_shape=jax.ShapeDtypeStruct((s_len, B_WIDTH), BF16),
        compiler_params=_cparams(("arbitrary", "arbitrary")),
        name="short_conv",
    )(h, h, h, h, h, h, h, conv_w.astype(F32))


KEY_TILE = 128


def _win_first_tile(i, tq, n_win, n_kt):
    return jnp.clip(i * (tq // KEY_TILE) - WINDOW // KEY_TILE, 0, n_kt - n_win)


def _win_attn_kernel(sink_ref, q_ref, *refs, tq, n_win, n_kt):
    klo_refs = refs[:n_win]
    khi_refs = refs[n_win:2 * n_win]
    vt_refs = refs[2 * n_win:3 * n_win]
    o_ref, s_ref, ot_ref = refs[3 * n_win:]
    i = pl.program_id(0)
    win = n_win * KEY_TILE
    hd = C_HEAD_DIM
    nt = (((1,), (1,)), ((), ()))
    q_start = i * tq
    k_start = _win_first_tile(i, tq, n_win, n_kt) * KEY_TILE
    kpos = k_start + lax.broadcasted_iota(jnp.int32, (win, tq), 0)
    qpos = q_start + lax.broadcasted_iota(jnp.int32, (win, tq), 1)
    bias = jnp.where(jnp.abs(qpos - kpos) <= WINDOW, 0.0, -jnp.inf)
    k_lo = jnp.concatenate([r[...] for r in klo_refs], axis=0)
    k_hi = jnp.concatenate([r[...] for r in khi_refs], axis=0)
    vt = jnp.concatenate([r[...] for r in vt_refs], axis=1)
    k_sel = (k_lo, k_hi)

    def scores(qh):
        pair, half = divmod(qh, 2)
        g = qh // C_GROUP
        qp = q_ref[:, pair * LANES:(pair + 1) * LANES]
        st = lax.dot_general(k_sel[half][:, g * LANES:(g + 1) * LANES], qp, nt, preferred_element_type=F32)
        s_ref[qh % 2] = st + bias

    def softmax_pv(qh):
        pair, half = divmod(qh, 2)
        g = qh // C_GROUP
        sink = sink_ref[qh]
        st = s_ref[qh % 2]
        m = jnp.maximum(jnp.max(st, axis=0, keepdims=True), sink)
        e = jnp.exp2(st - m)
        denom = jnp.sum(e, axis=0, keepdims=True) + jnp.exp2(sink - m)
        ot_ref[half * hd:(half + 1) * hd, :] = jnp.dot(vt[g * hd:(g + 1) * hd, :], e.astype(BF16),
                                                       preferred_element_type=F32) / denom
        if half == 1:
            o_ref[:, pair * LANES:(pair + 1) * LANES] = ot_ref[...].T.astype(o_ref.dtype)

    scores(0)
    for qh in range(C_Q_HEADS):
        if qh + 1 < C_Q_HEADS:
            scores(qh + 1)
        softmax_pv(qh)


def _win_attn(h, sinks_log2, *, tq):
    s_len = h.shape[0]
    tq = _tile(s_len, tq)
    n_win = (tq + 2 * WINDOW) // KEY_TILE
    n_kt = s_len // KEY_TILE
    assert n_kt >= n_win
    hd = C_HEAD_DIM
    k = h[:, C_Q_WIDTH:C_Q_WIDTH + C_KV_WIDTH].reshape(s_len, C_KV_HEADS, hd)
    zeros = jnp.zeros_like(k)
    k_lo = jnp.concatenate([k, zeros], axis=-1).reshape(s_len, C_KV_HEADS * LANES)
    k_hi = jnp.concatenate([zeros, k], axis=-1).reshape(s_len, C_KV_HEADS * LANES)
    vt = h[:, C_Q_WIDTH + C_KV_WIDTH:].reshape(n_kt, KEY_TILE, C_KV_WIDTH).transpose(0, 2, 1)

    def k_spec(j):
        return pl.BlockSpec((KEY_TILE, C_KV_HEADS * LANES),
                            lambda i: (_win_first_tile(i, tq, n_win, n_kt) + j, 0))

    def vt_spec(j):
        return pl.BlockSpec((None, C_KV_WIDTH, KEY_TILE),
                            lambda i: (_win_first_tile(i, tq, n_win, n_kt) + j, 0, 0))

    return pl.pallas_call(
        functools.partial(_win_attn_kernel, tq=tq, n_win=n_win, n_kt=n_kt),
        grid=(s_len // tq,),
        in_specs=[pl.BlockSpec(memory_space=pltpu.SMEM),
                  pl.BlockSpec((tq, C_Q_WIDTH), lambda i: (i, 0))]
                 + [k_spec(j) for j in range(n_win)] + [k_spec(j) for j in range(n_win)]
                 + [vt_spec(j) for j in range(n_win)],
        out_specs=pl.BlockSpec((tq, C_Q_WIDTH), lambda i: (i, 0)),
        out_shape=jax.ShapeDtypeStruct((s_len, C_Q_WIDTH), BF16),
        scratch_shapes=[pltpu.VMEM((2, n_win * KEY_TILE, tq), F32), pltpu.VMEM((2 * hd, tq), F32)],
        compiler_params=_cparams(("arbitrary",)),
        name="win_attn",
    )(sinks_log2, h, *([k_lo] * n_win), *([k_hi] * n_win), *([vt] * n_win))


def _layer_norm(z, g, b):
    mu = jnp.mean(z, axis=-1, keepdims=True)
    zc = z - mu
    var = jnp.mean(zc * zc, axis=-1, keepdims=True)
    return zc * lax.rsqrt(var + LN_EPS) * g + b


def _pack_bf16_pairs(z):
    n = z.shape[1] // 2
    lo = lax.bitcast_convert_type(z[:, :n].astype(jnp.bfloat16).astype(F32), jnp.int32)
    hi = lax.bitcast_convert_type(z[:, n:].astype(jnp.bfloat16).astype(F32), jnp.int32)
    return jnp.bitwise_or(hi, lax.shift_right_logical(lo, 16))


def _unpack_bf16_pairs(u):
    lo = lax.bitcast_convert_type(lax.shift_left(u, 16), F32).astype(BF16)
    hi = lax.bitcast_convert_type(jnp.bitwise_and(u, -65536), F32).astype(BF16)
    return jnp.concatenate([lo, hi], axis=1)


def _proj_out_ln_kernel(*refs, n_in, alpha):
    y_refs = refs[:n_in]
    w_ref, x_ref, g_ref, b_ref, xo_ref, xb_ref, xp_ref = refs[n_in:]
    acc = None
    row = 0
    for y_ref in y_refs:
        kw = y_ref.shape[1]
        part = jnp.dot(y_ref[...], w_ref[row:row + kw, :], preferred_element_type=F32)
        acc = part if acc is None else acc + part
        row += kw
    z = _layer_norm(alpha * x_ref[...] + acc, g_ref[...], b_ref[...])
    xo_ref[...] = z
    xb_ref[...] = z.astype(BF16)
    xp_ref[...] = _pack_bf16_pairs(z)


def _proj_out_ln(ys, w, x, g, b, *, alpha, tm):
    m, d = x.shape
    tm = _tile(m, tm)
    kdim = w.shape[0]
    in_specs = [pl.BlockSpec((tm, y.shape[1]), lambda i: (i, 0)) for y in ys]
    in_specs += [pl.BlockSpec((kdim, d), lambda i: (0, 0)),
                 pl.BlockSpec((tm, d), lambda i: (i, 0)),
                 pl.BlockSpec((1, d), lambda i: (0, 0)),
                 pl.BlockSpec((1, d), lambda i: (0, 0))]
    return pl.pallas_call(
        functools.partial(_proj_out_ln_kernel, n_in=len(ys), alpha=alpha),
        grid=(m // tm,),
        in_specs=in_specs,
        out_specs=[pl.BlockSpec((tm, d), lambda i: (i, 0)), pl.BlockSpec((tm, d), lambda i: (i, 0)),
                   pl.BlockSpec((tm, d // 2), lambda i: (i, 0))],
        out_shape=[jax.ShapeDtypeStruct((m, d), F32), jax.ShapeDtypeStruct((m, d), BF16),
                   jax.ShapeDtypeStruct((m, d // 2), jnp.int32)],
        compiler_params=_cparams(("arbitrary",)),
        name="proj_out_ln",
    )(*ys, w, x, g.reshape(1, d).astype(F32), b.reshape(1, d).astype(F32))


def _first_argmax(vals, iota, size):
    mx = jnp.max(vals, axis=0, keepdims=True)
    idx = jnp.min(jnp.where(vals == mx, iota, size), axis=0, keepdims=True)
    return mx, idx


def _router_kernel(x_ref, wt_ref, b_ref, eidx_ref, gate_ref, rank_ref, cnt_ref, carry_ref):
    i = pl.program_id(0)
    tm = x_ref.shape[0]
    neg = -jnp.inf

    @pl.when(i == 0)
    def _():
        carry_ref[...] = jnp.zeros(carry_ref.shape, F32)

    logits = lax.dot_general(wt_ref[...], x_ref[...], (((1,), (1,)), ((), ())),
                             precision=lax.Precision.HIGHEST, preferred_element_type=F32)
    scores = jax.nn.sigmoid(logits)
    choice = scores + b_ref[...]
    iota_m = lax.broadcasted_iota(jnp.int32, (EXPERTS_PER_GROUP, tm), 0)
    gs_rows = []
    for g in range(N_GROUPS):
        cg = choice[g * EXPERTS_PER_GROUP:(g + 1) * EXPERTS_PER_GROUP, :]
        m1, i1 = _first_argmax(cg, iota_m, EXPERTS_PER_GROUP)
        m2 = jnp.max(jnp.where(iota_m == i1, neg, cg), axis=0, keepdims=True)
        gs_rows.append(m1 + m2)
    gs = jnp.concatenate(gs_rows, axis=0)
    iota_g = lax.broadcasted_iota(jnp.int32, (N_GROUPS, tm), 0)
    sel = jnp.zeros((N_GROUPS, tm), F32)
    for _ in range(TOPK_GROUPS):
        _, gi = _first_argmax(gs, iota_g, N_GROUPS)
        hit = iota_g == gi
        sel = jnp.where(hit, 1.0, sel)
        gs = jnp.where(hit, neg, gs)
    allowed = jnp.concatenate(
        [jnp.broadcast_to(sel[g:g + 1, :], (EXPERTS_PER_GROUP, tm)) for g in range(N_GROUPS)], axis=0)
    iota_e = lax.broadcasted_iota(jnp.int32, (N_EXPERTS, tm), 0)
    masked = jnp.where(allowed > 0.0, choice, neg)
    onehot = jnp.zeros((N_EXPERTS, tm), F32)
    e_rows, w_rows = [], []
    for _ in range(TOP_K):
        _, ei = _first_argmax(masked, iota_e, N_EXPERTS)
        hit = iota_e == ei
        e_rows.append(ei)
        w_rows.append(jnp.sum(jnp.where(hit, scores, 0.0), axis=0, keepdims=True))
        onehot = jnp.where(hit, 1.0, onehot)
        masked = jnp.where(hit, neg, masked)
    wsel = jnp.concatenate(w_rows, axis=0)
    gate_ref[...] = wsel / jnp.sum(wsel, axis=0, keepdims=True) * ROUTE_SCALE
    eidx_ref[...] = jnp.concatenate(e_rows, axis=0)
    tri = (lax.broadcasted_iota(jnp.int32, (tm, tm), 0) < lax.broadcasted_iota(jnp.int32, (tm, tm), 1))
    prefix = jnp.dot(onehot.astype(BF16), tri.astype(BF16), preferred_element_type=F32)
    base = prefix + carry_ref[...]
    r_rows = [jnp.sum(jnp.where(iota_e == ei, base, 0.0), axis=0, keepdims=True) for ei in e_rows]
    rank_ref[...] = jnp.concatenate(r_rows, axis=0).astype(jnp.int32)
    carry_ref[...] = carry_ref[...] + jnp.sum(onehot, axis=1, keepdims=True)
    cnt_ref[...] = carry_ref[...].astype(jnp.int32)


def _router(x, router_w, router_b, *, tm):
    t, d = x.shape
    tm = _tile(t, tm)
    wt = router_w.astype(F32).T
    kt_spec = pl.BlockSpec((TOP_K, tm), lambda i: (0, i))
    return pl.pallas_call(
        _router_kernel,
        grid=(t // tm,),
        in_specs=[pl.BlockSpec((tm, d), lambda i: (i, 0)),
                  pl.BlockSpec((N_EXPERTS, d), lambda i: (0, 0)),
                  pl.BlockSpec((N_EXPERTS, 1), lambda i: (0, 0))],
        out_specs=[kt_spec, kt_spec, kt_spec, pl.BlockSpec((N_EXPERTS, 1), lambda i: (0, 0))],
        out_shape=[jax.ShapeDtypeStruct((TOP_K, t), jnp.int32), jax.ShapeDtypeStruct((TOP_K, t), F32),
                   jax.ShapeDtypeStruct((TOP_K, t), jnp.int32), jax.ShapeDtypeStruct((N_EXPERTS, 1), jnp.int32)],
        scratch_shapes=[pltpu.VMEM((N_EXPERTS, 1), F32)],
        compiler_params=_cparams(("arbitrary",)),
        name="router",
    )(x, wt, router_b.astype(F32).reshape(N_EXPERTS, 1))


def _dispatch_kernel(pos_ref, x_ref, xs_hbm, sem):
    tm = x_ref.shape[0]
    for t in range(tm):
        for k in range(TOP_K):
            pltpu.make_async_copy(x_ref.at[pl.ds(t, 1), :], xs_hbm.at[pl.ds(pos_ref[k, t], 1), :], sem).start()
    for k in range(TOP_K):
        pltpu.make_async_copy(x_ref, xs_hbm.at[pl.ds(0, tm), :], sem).wait()


def _dispatch(x, pos, rows, *, tm):
    t, d = x.shape
    tm = _tile(t, tm)
    return pl.pallas_call(
        _dispatch_kernel,
        grid=(t // tm,),
        in_specs=[pl.BlockSpec((TOP_K, tm), lambda i: (0, i), memory_space=pltpu.SMEM),
                  pl.BlockSpec((tm, d), lambda i: (i, 0))],
        out_specs=pl.BlockSpec(memory_space=pl.ANY),
        out_shape=jax.ShapeDtypeStruct((rows, d), x.dtype),
        scratch_shapes=[pltpu.SemaphoreType.DMA(())],
        compiler_params=_cparams(("arbitrary",)),
        name="dispatch",
    )(pos, x)


def _expert_weight_copies(w_hbm, stage, sems, layer, e):
    return [pltpu.make_async_copy(w.at[layer, e], st, sems.at[i]) for i, (w, st) in enumerate(zip(w_hbm, stage))]


def _gmm_kernel(blk_e_ref, nxt_e_ref, nused_ref, xs_ref, wg_hbm, wu_hbm, wd_hbm, o_ref,
                stage_g, stage_u, stage_d, wgu_s, wd_s, sems, *, layer, bm):
    step = pl.program_id(0)
    ff = stage_g.shape[1]
    copies = functools.partial(_expert_weight_copies, (wg_hbm, wu_hbm, wd_hbm), (stage_g, stage_u, stage_d), sems,
                               layer)

    @pl.when(step == 0)
    def _():
        for c in copies(blk_e_ref[0]):
            c.start()

    for sub in range(xs_ref.shape[0] // bm):
        b = step * (xs_ref.shape[0] // bm) + sub
        rows = slice(sub * bm, (sub + 1) * bm)
        used = b < nused_ref[0]
        new_e = jnp.logical_or(b == 0, blk_e_ref[b] != blk_e_ref[jnp.maximum(b - 1, 0)])

        @pl.when(jnp.logical_and(used, new_e))
        def _(b=b):
            for c in copies(blk_e_ref[b]):
                c.wait()
            for r in range(0, stage_g.shape[0], CAST_ROWS):
                wgu_s[r:r + CAST_ROWS, :ff] = stage_g[r:r + CAST_ROWS, :].astype(BF16)
                wgu_s[r:r + CAST_ROWS, ff:] = stage_u[r:r + CAST_ROWS, :].astype(BF16)
            for r in range(0, stage_d.shape[0], CAST_ROWS):
                wd_s[r:r + CAST_ROWS, :] = stage_d[r:r + CAST_ROWS, :].astype(BF16)

            @pl.when(nxt_e_ref[b] >= 0)
            def _():
                for c in copies(nxt_e_ref[b]):
                    c.start()

        @pl.when(used)
        def _(rows=rows):
            x = _unpack_bf16_pairs(xs_ref[rows, :])
            gu = jnp.dot(x, wgu_s[...], preferred_element_type=F32)
            gate = gu[:, :ff]
            hmid = gate * jax.nn.sigmoid(gate) * gu[:, ff:]
            o_ref[rows, :] = _pack_bf16_pairs(jnp.dot(hmid.astype(BF16), wd_s[...], preferred_element_type=F32))


def _gmm(xs, w_gate, w_up, w_down, layer, blk_e, nxt_e, nused, *, bm):
    rows = xs.shape[0]
    d, ff = w_gate.shape[2], w_gate.shape[3]
    n_blocks = rows // bm
    per_step = GMM_BLOCKS_PER_STEP
    assert n_blocks % per_step == 0

    def blk(s, be, nx, nu):
        return (jnp.minimum(s, (nu[0] - 1) // per_step), 0)

    grid_spec = pltpu.PrefetchScalarGridSpec(
        num_scalar_prefetch=3,
        grid=(n_blocks // per_step,),
        in_specs=[pl.BlockSpec((per_step * bm, d // 2), blk),
                  pl.BlockSpec(memory_space=pl.ANY), pl.BlockSpec(memory_space=pl.ANY),
                  pl.BlockSpec(memory_space=pl.ANY)],
        out_specs=pl.BlockSpec((per_step * bm, d // 2), blk),
        scratch_shapes=[pltpu.VMEM((d, ff), F32), pltpu.VMEM((d, ff), F32), pltpu.VMEM((ff, d), F32),
                        pltpu.VMEM((d, 2 * ff), BF16), pltpu.VMEM((ff, d), BF16),
                        pltpu.SemaphoreType.DMA((3,))],
    )
    return pl.pallas_call(
        functools.partial(_gmm_kernel, layer=layer, bm=bm),
        grid_spec=grid_spec,
        out_shape=jax.ShapeDtypeStruct((rows, d // 2), jnp.int32),
        compiler_params=_cparams(("arbitrary",)),
        name="gmm",
    )(blk_e, nxt_e, nused, xs, w_gate, w_up, w_down)


def _combine_kernel(pos_ref, pos_next_ref, os_hbm, gate_ref, x_ref, xb_ref, wgu_ref, wd_ref, g_ref, b_ref,
                    xo_ref, xbo_ref, buf_a, buf_b, y_ref, sems, *, alpha):
    i = pl.program_id(0)
    n_i = pl.num_programs(0)
    tm = buf_a.shape[1]
    ff = wd_ref.shape[0]
    half = buf_a.shape[2]
    bufs = (buf_a, buf_b)

    def row_copy(p, s, k, t):
        return pltpu.make_async_copy(os_hbm.at[pl.ds(p, 1), :], bufs[s].at[k, pl.ds(t, 1), :], sems.at[s])

    def issue(p_ref, col0, s):
        for t in range(tm):
            for k in range(TOP_K):
                row_copy(p_ref[k, col0 + t], s, k, t).start()

    def wait_slot(s):
        for k in range(TOP_K):
            pltpu.make_async_copy(os_hbm.at[pl.ds(0, tm), :], bufs[s].at[k], sems.at[s]).wait()

    def finish(r0, s):
        rows = slice(r0, r0 + tm)
        wait_slot(s)
        lo = jnp.zeros((tm, half), F32)
        hi = jnp.zeros((tm, half), F32)
        for k in range(TOP_K):
            u = bufs[s][k]
            w = gate_ref[rows, k:k + 1]
            lo = lo + lax.bitcast_convert_type(lax.shift_left(u, 16), F32) * w
            hi = hi + lax.bitcast_convert_type(jnp.bitwise_and(u, -65536), F32) * w
        routed = jnp.concatenate([lo, hi], axis=1)
        z = _layer_norm(alpha * x_ref[rows, :] + (routed + y_ref[rows, :]), g_ref[...], b_ref[...])
        xo_ref[rows, :] = z
        xbo_ref[rows, :] = z.astype(BF16)

    @pl.when(i == 0)
    def _():
        def body(t, carry):
            for k in range(TOP_K):
                row_copy(pos_ref[k, t], 0, k, t).start()
            return carry
        lax.fori_loop(0, tm, body, 0)

    issue(pos_ref, tm, 1)
    gu = jnp.dot(xb_ref[...], wgu_ref[...], preferred_element_type=F32)
    gate = gu[:, :ff]
    hmid = gate * jax.nn.sigmoid(gate) * gu[:, ff:]
    y_ref[...] = jnp.dot(hmid.astype(BF16), wd_ref[...], preferred_element_type=F32)
    finish(0, 0)
    issue(pos_next_ref, 0, 0)
    finish(tm, 1)

    @pl.when(i == n_i - 1)
    def _():
        wait_slot(0)


def _combine(os_, pos, gate_t, x, xb, sh_wgu, sh_wd, g, b, *, alpha, tm):
    t, d = x.shape
    tm = _tile(t // 2, tm)
    ff = sh_wd.shape[0]
    n_i = t // (2 * tm)
    row_spec = pl.BlockSpec((2 * tm, d), lambda i: (i, 0))
    vec_spec = pl.BlockSpec((1, d), lambda i: (0, 0))
    return pl.pallas_call(
        functools.partial(_combine_kernel, alpha=alpha),
        grid=(n_i,),
        in_specs=[pl.BlockSpec((TOP_K, 2 * tm), lambda i: (0, i), memory_space=pltpu.SMEM),
                  pl.BlockSpec((TOP_K, 2 * tm), lambda i: (0, jnp.minimum(i + 1, n_i - 1)), memory_space=pltpu.SMEM),
                  pl.BlockSpec(memory_space=pl.ANY),
                  pl.BlockSpec((2 * tm, TOP_K), lambda i: (i, 0)),
                  row_spec, row_spec,
                  pl.BlockSpec((d, 2 * ff), lambda i: (0, 0)),
                  pl.BlockSpec((ff, d), lambda i: (0, 0)),
                  vec_spec, vec_spec],
        out_specs=[row_spec, row_spec],
        out_shape=[jax.ShapeDtypeStruct((t, d), F32), jax.ShapeDtypeStruct((t, d), BF16)],
        scratch_shapes=[pltpu.VMEM((TOP_K, tm, d // 2), jnp.int32), pltpu.VMEM((TOP_K, tm, d // 2), jnp.int32),
                        pltpu.VMEM((2 * tm, d), F32), pltpu.SemaphoreType.DMA((2,))],
        compiler_params=_cparams(("arbitrary",)),
        name="combine",
    )(pos, pos, os_, gate_t, x, xb, sh_wgu, sh_wd, g.reshape(1, d).astype(F32), b.reshape(1, d).astype(F32))


def _moe(x, xb, xp, router_w, router_b, w_gate, w_up, w_down, layer, sh_wg, sh_wu, sh_wd, g, b, *, alpha):
    t, d = x.shape
    bm = GMM_BLOCK
    eidx, gate, rank, counts = _router(x, router_w, router_b, tm=512)
    counts = counts.reshape(N_EXPERTS)
    padded = (counts + bm - 1) // bm * bm
    pad_end = jnp.cumsum(padded)
    pad_start = pad_end - padded
    n_blocks = t * TOP_K // bm + N_EXPERTS
    nused = (pad_end[-1] // bm).astype(jnp.int32).reshape(1)
    blk_start = jnp.minimum(jnp.arange(n_blocks, dtype=jnp.int32), nused - 1) * bm
    blk_e = jnp.sum((pad_end[None, :] <= blk_start[:, None]).astype(jnp.int32), axis=1)
    blk_e = jnp.minimum(blk_e, N_EXPERTS - 1).astype(jnp.int32)
    e_ids = jnp.arange(N_EXPERTS, dtype=jnp.int32)
    later_used = jnp.logical_and(e_ids[None, :] > e_ids[:, None], counts[None, :] > 0)
    next_used = jnp.min(jnp.where(later_used, e_ids[None, :], N_EXPERTS), axis=1)
    next_used = jnp.where(next_used == N_EXPERTS, -1, next_used).astype(jnp.int32)
    nxt_e = jnp.sum(jnp.where(blk_e[:, None] == e_ids[None, :], next_used[None, :], 0), axis=1).astype(jnp.int32)
    start_of = jnp.sum(jnp.where(eidx[:, :, None] == e_ids, pad_start, 0), axis=-1)
    pos = (start_of + rank).astype(jnp.int32)
    xs = _dispatch(xp, pos, n_blocks * bm, tm=256)
    os_ = _gmm(xs, w_gate, w_up, w_down, layer, blk_e, nxt_e, nused, bm=bm)
    sh_wgu = jnp.concatenate([sh_wg, sh_wu], axis=1).astype(BF16)
    return _combine(os_, pos, gate.T, x, xb, sh_wgu, sh_wd.astype(BF16), g, b, alpha=alpha, tm=128)


def kernel(x, positions, even_w_in, even_w_out, a_lambda, a_subln_g, b_conv_w, odd_w_in, odd_w_out, c_sinks,
           ln1_g, ln1_b, ln2_g, ln2_b, router_w, router_b, exp_w_gate, exp_w_up, exp_w_down,
           sh_w_gate, sh_w_up, sh_w_down):
    bn, s_len, d = x.shape
    depth = ln1_g.shape[0]
    alpha = (2.0 * depth) ** 0.25
    outs = []
    for bi in range(bn):
        xf = x[bi]
        xb = xf
        tab_a = _rope_tables(positions[bi], A_QK_DIM, A_ROT_DIM)
        tab_c = _rope_tables(positions[bi], C_HEAD_DIM, C_ROT_DIM)
        for layer in range(depth):
            j = layer // 2
            if layer % 2 == 0:
                h = _proj_in(xb, even_w_in, j, tab_a, q_width=A_QK_WIDTH, k_width=A_QK_WIDTH,
                             half=A_ROT_DIM // 2, q_scale=A_QK_DIM ** -0.5 * LOG2E, tm=1024, tn=512)
                lam_init = 0.8 - 0.6 * math.exp(-0.3 * layer)
                lv = a_lambda[j].astype(F32)
                lam = (jnp.exp(jnp.sum(lv[0] * lv[1])) - jnp.exp(jnp.sum(lv[2] * lv[3])) + lam_init).reshape(1)
                y_a = _diff_attn(h, lam, a_subln_g[j], lam_init=lam_init, tq=2048, tk=512)
                y_b = _short_conv(h, b_conv_w[j], tr=1024, tc=512)
                ys = [y_a, y_b]
                w_out = even_w_out[j]
            else:
                h = _proj_in(xb, odd_w_in, j, tab_c, q_width=C_Q_WIDTH, k_width=C_KV_WIDTH,
                             half=C_ROT_DIM // 2, q_scale=C_HEAD_DIM ** -0.5 * LOG2E, tm=1024, tn=512)
                ys = [_win_attn(h, c_sinks[j].astype(F32) * LOG2E, tq=256)]
                w_out = odd_w_out[j]
            xf, xb, xp = _proj_out_ln(ys, w_out.astype(BF16), xf, ln1_g[layer], ln1_b[layer], alpha=alpha, tm=256)
            xf, xb = _moe(xf, xb, xp, router_w[layer], router_b[layer], exp_w_gate, exp_w_up, exp_w_down, layer,
                          sh_w_gate[layer], sh_w_up[layer], sh_w_down[layer],
                          ln2_g[layer], ln2_b[layer], alpha=alpha)
        outs.append(xf)
    return jnp.stack(outs, axis=0)
```

```python
import functools
import math

import jax
import jax.numpy as jnp
from jax import lax
from jax.experimental import pallas as pl
from jax.experimental.pallas import tpu as pltpu

F32 = jnp.float32
BF16 = jnp.bfloat16

LANES = 128
MXU_CHUNKS = 2
VMEM_LIMIT_BYTES = 56 * 1024 * 1024

A_HEADS = 4
A_QK_DIM = 128
A_V_DIM = 256
A_QK_WIDTH = 1024
A_WIDTH = 1024
B_WIDTH = 1024
C_Q_HEADS = 32
C_KV_HEADS = 4
C_HEAD_DIM = 64
C_GROUP = C_Q_HEADS // C_KV_HEADS
C_Q_WIDTH = C_Q_HEADS * C_HEAD_DIM
C_KV_WIDTH = C_KV_HEADS * C_HEAD_DIM
WINDOW = 128
ROPE_THETA = 500000.0
A_ROT_DIM = A_QK_DIM // 4
C_ROT_DIM = C_HEAD_DIM // 4
N_EXPERTS = 64
TOP_K = 8
N_GROUPS = 8
TOPK_GROUPS = 4
EXPERTS_PER_GROUP = N_EXPERTS // N_GROUPS
ROUTE_SCALE = 2.5
LN_EPS = 1e-5
LOG2E = math.log2(math.e)

GMM_BLOCK = 256
GMM_BLOCKS_PER_STEP = 4
CAST_ROWS = 64


def _cparams(semantics):
    return pltpu.CompilerParams(dimension_semantics=semantics, vmem_limit_bytes=VMEM_LIMIT_BYTES)


def _tile(dim, want):
    t = min(dim, want)
    assert dim % t == 0, (dim, want)
    return t


def _rope_tables(positions, head_dim, rot_dim):
    half = rot_dim // 2
    inv_freq = ROPE_THETA ** (-jnp.arange(half, dtype=F32) * 2.0 / rot_dim)
    ang = positions.astype(F32)[:, None] * inv_freq
    cos, sin = jnp.cos(ang), jnp.sin(ang)
    lane = jnp.arange(LANES) % head_dim
    idx = lane % half
    cos_l = jnp.take(cos, idx, axis=1)
    sin_l = jnp.take(sin, idx, axis=1)
    c = jnp.where(lane < 2 * half, cos_l, 1.0)
    s1 = jnp.where(lane < half, -sin_l, 0.0)
    s2 = jnp.where((lane >= half) & (lane < 2 * half), sin_l, 0.0)
    return c.astype(F32), s1.astype(F32), s2.astype(F32)


def _proj_in_kernel(x_ref, w_ref, c_ref, s1_ref, s2_ref, o_ref, wb_ref, *, tile_kinds, half, q_scale):
    j = pl.program_id(0)

    @pl.when(pl.program_id(1) == 0)
    def _():
        wb_ref[...] = w_ref[...].astype(BF16)

    n_chunks = o_ref.shape[1] // LANES
    groups = {}
    for t, kinds in enumerate(tile_kinds):
        groups.setdefault(kinds, []).append(t)
    for kinds, tiles in groups.items():
        cond = functools.reduce(jnp.logical_or, [j == t for t in tiles])

        @pl.when(cond)
        def _(kinds=kinds):
            x = x_ref[...].astype(BF16)
            for c in range(n_chunks):
                if c % MXU_CHUNKS == 0:
                    wide = jnp.dot(x, wb_ref[:, c * LANES:(c + MXU_CHUNKS) * LANES], preferred_element_type=F32)
                r = wide[:, (c % MXU_CHUNKS) * LANES:(c % MXU_CHUNKS + 1) * LANES]
                if kinds[c] != "n":
                    r = (r * c_ref[...] + pltpu.roll(r, LANES - half, 1) * s1_ref[...]
                         + pltpu.roll(r, half, 1) * s2_ref[...])
                    if kinds[c] == "q":
                        r = r * q_scale
                o_ref[:, c * LANES:(c + 1) * LANES] = r.astype(o_ref.dtype)


def _proj_in(x, w_stack, w_idx, tables, *, q_width, k_width, half, q_scale, tm, tn):
    m, kdim = x.shape
    n = w_stack.shape[2]
    tm = _tile(m, tm)
    tn = _tile(n, tn)

    def kind(col):
        return "q" if col < q_width else ("k" if col < q_width + k_width else "n")

    tile_kinds = tuple(tuple(kind(t * tn + c * LANES) for c in range(tn // LANES)) for t in range(n // tn))
    c, s1, s2 = tables
    tab_spec = pl.BlockSpec((tm, LANES), lambda j, i: (i, 0))
    return pl.pallas_call(
        functools.partial(_proj_in_kernel, tile_kinds=tile_kinds, half=half, q_scale=q_scale),
        grid=(n // tn, m // tm),
        in_specs=[pl.BlockSpec((tm, kdim), lambda j, i: (i, 0)),
                  pl.BlockSpec((None, kdim, tn), lambda j, i: (w_idx, 0, j)),
                  tab_spec, tab_spec, tab_spec],
        out_specs=pl.BlockSpec((tm, tn), lambda j, i: (i, j)),
        out_shape=jax.ShapeDtypeStruct((m, n), BF16),
        scratch_shapes=[pltpu.VMEM((kdim, tn), BF16)],
        compiler_params=_cparams(("arbitrary", "arbitrary")),
        name="proj_in",
    )(x, w_stack, c, s1, s2)


def _diff_attn_kernel(lam_ref, q_ref, k_ref, vt_ref, g_ref, o_ref, m_ref, l_ref, acc_ref, s_ref, *, tk, lam_init):
    n_kt = vt_ref.shape[0]
    dk = A_QK_DIM
    nt = (((1,), (1,)), ((), ()))
    m_ref[...] = jnp.full(m_ref.shape, -jnp.inf, F32)
    l_ref[...] = jnp.zeros(l_ref.shape, F32)
    acc_ref[...] = jnp.zeros(acc_ref.shape, F32)

    def scores(kk, slot):
        ks = pl.multiple_of(kk * tk, tk)
        for mi in range(2):
            kb = k_ref[pl.ds(ks, tk), mi * dk:(mi + 1) * dk]
            qm = q_ref[:, mi * dk:(mi + 1) * dk]
            s_ref[slot, mi] = lax.dot_general(kb, qm, nt, preferred_element_type=F32)

    def softmax_pv(kk, slot):
        vt = vt_ref[kk]
        for mi in range(2):
            st = s_ref[slot, mi]
            m_old = m_ref[mi]
            m_new = jnp.maximum(m_old, jnp.max(st, axis=0, keepdims=True))
            alpha = jnp.exp2(m_old - m_new)
            pt = jnp.exp2(st - m_new)
            l_ref[mi] = alpha * l_ref[mi] + jnp.sum(pt, axis=0, keepdims=True)
            acc_ref[mi] = alpha * acc_ref[mi] + jnp.dot(vt, pt.astype(BF16), preferred_element_type=F32)
            m_ref[mi] = m_new

    scores(0, 0)

    def body(kk2, carry):
        kk = kk2 * 2
        scores(kk + 1, 1)
        softmax_pv(kk, 0)
        scores(jnp.minimum(kk + 2, n_kt - 1), 0)
        softmax_pv(kk + 1, 1)
        return carry

    lax.fori_loop(0, n_kt // 2, body, 0)
    lam = lam_ref[0]
    o = acc_ref[0] / l_ref[0] - lam * (acc_ref[1] / l_ref[1])
    ms = jnp.mean(o * o, axis=0, keepdims=True)
    o = o * lax.rsqrt(ms + LN_EPS) * g_ref[...] * (1.0 - lam_init)
    o_ref[...] = o.T.astype(o_ref.dtype)


def _diff_attn(h, lam, subln_g, *, lam_init, tq, tk):
    s_len = h.shape[0]
    tq = _tile(s_len, tq)
    tk = _tile(s_len, tk)
    w = 2 * A_QK_DIM
    k_off = A_QK_WIDTH // w
    n_kt = s_len // tk
    vt = h[:, 2 * A_QK_WIDTH:2 * A_QK_WIDTH + A_WIDTH].reshape(n_kt, tk, A_HEADS, A_V_DIM).transpose(2, 0, 3, 1)
    return pl.pallas_call(
        functools.partial(_diff_attn_kernel, tk=tk, lam_init=lam_init),
        grid=(A_HEADS, s_len // tq),
        in_specs=[pl.BlockSpec(memory_space=pltpu.SMEM),
                  pl.BlockSpec((tq, w), lambda hh, i: (i, hh)),
                  pl.BlockSpec((s_len, w), lambda hh, i: (0, k_off + hh)),
                  pl.BlockSpec((None, n_kt, A_V_DIM, tk), lambda hh, i: (hh, 0, 0, 0)),
                  pl.BlockSpec((A_V_DIM, 1), lambda hh, i: (0, 0))],
        out_specs=pl.BlockSpec((tq, A_V_DIM), lambda hh, i: (i, hh)),
        out_shape=jax.ShapeDtypeStruct((s_len, A_WIDTH), BF16),
        scratch_shapes=[pltpu.VMEM((2, 1, tq), F32), pltpu.VMEM((2, 1, tq), F32),
                        pltpu.VMEM((2, A_V_DIM, tq), F32), pltpu.VMEM((2, 2, tk, tq), F32)],
        compiler_params=_cparams(("arbitrary", "arbitrary")),
        name="diff_attn",
    )(lam, h, h, vt, subln_g.reshape(A_V_DIM, 1).astype(F32))


HALO = 16


def _short_conv_kernel(bg_ref, cg_ref, hb_ref, cgp_ref, hbp_ref, cgn_ref, hbn_ref, w_ref, o_ref):
    i = pl.program_id(0)
    n_i = pl.num_programs(0)
    u = cg_ref[...].astype(F32) * hb_ref[...].astype(F32)
    tr = u.shape[0]
    prev_row = cgp_ref[HALO - 1:HALO, :].astype(F32) * hbp_ref[HALO - 1:HALO, :].astype(F32)
    next_row = cgn_ref[0:1, :].astype(F32) * hbn_ref[0:1, :].astype(F32)
    prev_row = jnp.where(i == 0, 0.0, prev_row)
    next_row = jnp.where(i == n_i - 1, 0.0, next_row)
    row = lax.broadcasted_iota(jnp.int32, u.shape, 0)
    u_prev = jnp.where(row == 0, prev_row, pltpu.roll(u, 1, 0))
    u_next = jnp.where(row == tr - 1, next_row, pltpu.roll(u, tr - 1, 0))
    w = w_ref[...]
    y = w[0:1, :] * u_prev + w[1:2, :] * u + w[2:3, :] * u_next
    o_ref[...] = (bg_ref[...].astype(F32) * y).astype(o_ref.dtype)


def _short_conv(h, conv_w, *, tr, tc):
    s_len = h.shape[0]
    tr = _tile(s_len, tr)
    tc = _tile(B_WIDTH, tc)
    base = 2 * A_QK_WIDTH + A_WIDTH
    bg_off, cg_off, hb_off = base // tc, (base + B_WIDTH) // tc, (base + 2 * B_WIDTH) // tc
    rb = tr // HALO
    last = s_len // HALO - 1

    def main(off):
        return pl.BlockSpec((tr, tc), lambda i, c: (i, off + c))

    def prev(off):
        return pl.BlockSpec((HALO, tc), lambda i, c: (jnp.maximum(i * rb - 1, 0), off + c))

    def nxt(off):
        return pl.BlockSpec((HALO, tc), lambda i, c: (jnp.minimum((i + 1) * rb, last), off + c))

    return pl.pallas_call(
        _short_conv_kernel,
        grid=(s_len // tr, B_WIDTH // tc),
        in_specs=[main(bg_off), main(cg_off), main(hb_off), prev(cg_off), prev(hb_off), nxt(cg_off), nxt(hb_off),
                  pl.BlockSpec((3, tc), lambda i, c: (0, c))],
        out_specs=pl.BlockSpec((tr, tc), lambda i, c: (i, c)),
        out_shape=jax.ShapeDtypeStruct((s_len, B_WIDTH), BF16),
        compiler_params=_cparams(("arbitrary", "arbitrary")),
        name="short_conv",
    )(h, h, h, h, h, h, h, conv_w.astype(F32))


KEY_TILE = 128


def _win_first_tile(i, tq, n_win, n_kt):
    return jnp.clip(i * (tq // KEY_TILE) - WINDOW // KEY_TILE, 0, n_kt - n_win)


def _win_attn_kernel(sink_ref, q_ref, *refs, tq, n_win, n_kt):
    klo_refs = refs[:n_win]
    khi_refs = refs[n_win:2 * n_win]
    vt_refs = refs[2 * n_win:3 * n_win]
    o_ref, s_ref, ot_ref = refs[3 * n_win:]
    i = pl.program_id(0)
    win = n_win * KEY_TILE
    hd = C_HEAD_DIM
    nt = (((1,), (1,)), ((), ()))
    q_start = i * tq
    k_start = _win_first_tile(i, tq, n_win, n_kt) * KEY_TILE
    kpos = k_start + lax.broadcasted_iota(jnp.int32, (win, tq), 0)
    qpos = q_start + lax.broadcasted_iota(jnp.int32, (win, tq), 1)
    bias = jnp.where(jnp.abs(qpos - kpos) <= WINDOW, 0.0, -jnp.inf)
    k_lo = jnp.concatenate([r[...] for r in klo_refs], axis=0)
    k_hi = jnp.concatenate([r[...] for r in khi_refs], axis=0)
    vt = jnp.concatenate([r[...] for r in vt_refs], axis=1)
    k_sel = (k_lo, k_hi)

    def scores(qh):
        pair, half = divmod(qh, 2)
        g = qh // C_GROUP
        qp = q_ref[:, pair * LANES:(pair + 1) * LANES]
        st = lax.dot_general(k_sel[half][:, g * LANES:(g + 1) * LANES], qp, nt, preferred_element_type=F32)
        s_ref[qh % 2] = st + bias

    def softmax_pv(qh):
        pair, half = divmod(qh, 2)
        g = qh // C_GROUP
        sink = sink_ref[qh]
        st = s_ref[qh % 2]
        m = jnp.maximum(jnp.max(st, axis=0, keepdims=True), sink)
        e = jnp.exp2(st - m)
        denom = jnp.sum(e, axis=0, keepdims=True) + jnp.exp2(sink - m)
        ot_ref[half * hd:(half + 1) * hd, :] = jnp.dot(vt[g * hd:(g + 1) * hd, :], e.astype(BF16),
                                                       preferred_element_type=F32) / denom
        if half == 1:
            o_ref[:, pair * LANES:(pair + 1) * LANES] = ot_ref[...].T.astype(o_ref.dtype)

    scores(0)
    for qh in range(C_Q_HEADS):
        if qh + 1 < C_Q_HEADS:
            scores(qh + 1)
        softmax_pv(qh)


def _win_attn(h, sinks_log2, *, tq):
    s_len = h.shape[0]
    tq = _tile(s_len, tq)
    n_win = (tq + 2 * WINDOW) // KEY_TILE
    n_kt = s_len // KEY_TILE
    assert n_kt >= n_win
    hd = C_HEAD_DIM
    k = h[:, C_Q_WIDTH:C_Q_WIDTH + C_KV_WIDTH].reshape(s_len, C_KV_HEADS, hd)
    zeros = jnp.zeros_like(k)
    k_lo = jnp.concatenate([k, zeros], axis=-1).reshape(s_len, C_KV_HEADS * LANES)
    k_hi = jnp.concatenate([zeros, k], axis=-1).reshape(s_len, C_KV_HEADS * LANES)
    vt = h[:, C_Q_WIDTH + C_KV_WIDTH:].reshape(n_kt, KEY_TILE, C_KV_WIDTH).transpose(0, 2, 1)

    def k_spec(j):
        return pl.BlockSpec((KEY_TILE, C_KV_HEADS * LANES),
                            lambda i: (_win_first_tile(i, tq, n_win, n_kt) + j, 0))

    def vt_spec(j):
        return pl.BlockSpec((None, C_KV_WIDTH, KEY_TILE),
                            lambda i: (_win_first_tile(i, tq, n_win, n_kt) + j, 0, 0))

    return pl.pallas_call(
        functools.partial(_win_attn_kernel, tq=tq, n_win=n_win, n_kt=n_kt),
        grid=(s_len // tq,),
        in_specs=[pl.BlockSpec(memory_space=pltpu.SMEM),
                  pl.BlockSpec((tq, C_Q_WIDTH), lambda i: (i, 0))]
                 + [k_spec(j) for j in range(n_win)] + [k_spec(j) for j in range(n_win)]
                 + [vt_spec(j) for j in range(n_win)],
        out_specs=pl.BlockSpec((tq, C_Q_WIDTH), lambda i: (i, 0)),
        out_shape=jax.ShapeDtypeStruct((s_len, C_Q_WIDTH), BF16),
        scratch_shapes=[pltpu.VMEM((2, n_win * KEY_TILE, tq), F32), pltpu.VMEM((2 * hd, tq), F32)],
        compiler_params=_cparams(("arbitrary",)),
        name="win_attn",
    )(sinks_log2, h, *([k_lo] * n_win), *([k_hi] * n_win), *([vt] * n_win))


def _layer_norm(z, g, b):
    mu = jnp.mean(z, axis=-1, keepdims=True)
    zc = z - mu
    var = jnp.mean(zc * zc, axis=-1, keepdims=True)
    return zc * lax.rsqrt(var + LN_EPS) * g + b


def _pack_bf16_pairs(z):
    n = z.shape[1] // 2
    lo = lax.bitcast_convert_type(z[:, :n].astype(jnp.bfloat16).astype(F32), jnp.int32)
    hi = lax.bitcast_convert_type(z[:, n:].astype(jnp.bfloat16).astype(F32), jnp.int32)
    return jnp.bitwise_or(hi, lax.shift_right_logical(lo, 16))


def _unpack_bf16_pairs(u):
    lo = lax.bitcast_convert_type(lax.shift_left(u, 16), F32).astype(BF16)
    hi = lax.bitcast_convert_type(jnp.bitwise_and(u, -65536), F32).astype(BF16)
    return jnp.concatenate([lo, hi], axis=1)


def _proj_out_ln_kernel(*refs, n_in, alpha):
    y_refs = refs[:n_in]
    w_ref, x_ref, g_ref, b_ref, xo_ref, xb_ref, xp_ref = refs[n_in:]
    acc = None
    row = 0
    for y_ref in y_refs:
        kw = y_ref.shape[1]
        part = jnp.dot(y_ref[...], w_ref[row:row + kw, :], preferred_element_type=F32)
        acc = part if acc is None else acc + part
        row += kw
    z = _layer_norm(alpha * x_ref[...] + acc, g_ref[...], b_ref[...])
    xo_ref[...] = z
    xb_ref[...] = z.astype(BF16)
    xp_ref[...] = _pack_bf16_pairs(z)


def _proj_out_ln(ys, w, x, g, b, *, alpha, tm):
    m, d = x.shape
    tm = _tile(m, tm)
    kdim = w.shape[0]
    in_specs = [pl.BlockSpec((tm, y.shape[1]), lambda i: (i, 0)) for y in ys]
    in_specs += [pl.BlockSpec((kdim, d), lambda i: (0, 0)),
                 pl.BlockSpec((tm, d), lambda i: (i, 0)),
                 pl.BlockSpec((1, d), lambda i: (0, 0)),
                 pl.BlockSpec((1, d), lambda i: (0, 0))]
    return pl.pallas_call(
        functools.partial(_proj_out_ln_kernel, n_in=len(ys), alpha=alpha),
        grid=(m // tm,),
        in_specs=in_specs,
        out_specs=[pl.BlockSpec((tm, d), lambda i: (i, 0)), pl.BlockSpec((tm, d), lambda i: (i, 0)),
                   pl.BlockSpec((tm, d // 2), lambda i: (i, 0))],
        out_shape=[jax.ShapeDtypeStruct((m, d), F32), jax.ShapeDtypeStruct((m, d), BF16),
                   jax.ShapeDtypeStruct((m, d // 2), jnp.int32)],
        compiler_params=_cparams(("arbitrary",)),
        name="proj_out_ln",
    )(*ys, w, x, g.reshape(1, d).astype(F32), b.reshape(1, d).astype(F32))


def _first_argmax(vals, iota, size):
    mx = jnp.max(vals, axis=0, keepdims=True)
    idx = jnp.min(jnp.where(vals == mx, iota, size), axis=0, keepdims=True)
    return mx, idx


def _router_kernel(x_ref, wt_ref, b_ref, eidx_ref, gate_ref, rank_ref, cnt_ref, carry_ref):
    i = pl.program_id(0)
    tm = x_ref.shape[0]
    neg = -jnp.inf

    @pl.when(i == 0)
    def _():
        carry_ref[...] = jnp.zeros(carry_ref.shape, F32)

    logits = lax.dot_general(wt_ref[...], x_ref[...], (((1,), (1,)), ((), ())),
                             precision=lax.Precision.HIGHEST, preferred_element_type=F32)
    scores = jax.nn.sigmoid(logits)
    choice = scores + b_ref[...]
    iota_m = lax.broadcasted_iota(jnp.int32, (EXPERTS_PER_GROUP, tm), 0)
    gs_rows = []
    for g in range(N_GROUPS):
        cg = choice[g * EXPERTS_PER_GROUP:(g + 1) * EXPERTS_PER_GROUP, :]
        m1, i1 = _first_argmax(cg, iota_m, EXPERTS_PER_GROUP)
        m2 = jnp.max(jnp.where(iota_m == i1, neg, cg), axis=0, keepdims=True)
        gs_rows.append(m1 + m2)
    gs = jnp.concatenate(gs_rows, axis=0)
    iota_g = lax.broadcasted_iota(jnp.int32, (N_GROUPS, tm), 0)
    sel = jnp.zeros((N_GROUPS, tm), F32)
    for _ in range(TOPK_GROUPS):
        _, gi = _first_argmax(gs, iota_g, N_GROUPS)
        hit = iota_g == gi
        sel = jnp.where(hit, 1.0, sel)
        gs = jnp.where(hit, neg, gs)
    allowed = jnp.concatenate(
        [jnp.broadcast_to(sel[g:g + 1, :], (EXPERTS_PER_GROUP, tm)) for g in range(N_GROUPS)], axis=0)
    iota_e = lax.broadcasted_iota(jnp.int32, (N_EXPERTS, tm), 0)
    masked = jnp.where(allowed > 0.0, choice, neg)
    onehot = jnp.zeros((N_EXPERTS, tm), F32)
    e_rows, w_rows = [], []
    for _ in range(TOP_K):
        _, ei = _first_argmax(masked, iota_e, N_EXPERTS)
        hit = iota_e == ei
        e_rows.append(ei)
        w_rows.append(jnp.sum(jnp.where(hit, scores, 0.0), axis=0, keepdims=True))
        onehot = jnp.where(hit, 1.0, onehot)
        masked = jnp.where(hit, neg, masked)
    wsel = jnp.concatenate(w_rows, axis=0)
    gate_ref[...] = wsel / jnp.sum(wsel, axis=0, keepdims=True) * ROUTE_SCALE
    eidx_ref[...] = jnp.concatenate(e_rows, axis=0)
    tri = (lax.broadcasted_iota(jnp.int32, (tm, tm), 0) < lax.broadcasted_iota(jnp.int32, (tm, tm), 1))
    prefix = jnp.dot(onehot.astype(BF16), tri.astype(BF16), preferred_element_type=F32)
    base = prefix + carry_ref[...]
    r_rows = [jnp.sum(jnp.where(iota_e == ei, base, 0.0), axis=0, keepdims=True) for ei in e_rows]
    rank_ref[...] = jnp.concatenate(r_rows, axis=0).astype(jnp.int32)
    carry_ref[...] = carry_ref[...] + jnp.sum(onehot, axis=1, keepdims=True)
    cnt_ref[...] = carry_ref[...].astype(jnp.int32)


def _router(x, router_w, router_b, *, tm):
    t, d = x.shape
    tm = _tile(t, tm)
    wt = router_w.astype(F32).T
    kt_spec = pl.BlockSpec((TOP_K, tm), lambda i: (0, i))
    return pl.pallas_call(
        _router_kernel,
        grid=(t // tm,),
        in_specs=[pl.BlockSpec((tm, d), lambda i: (i, 0)),
                  pl.BlockSpec((N_EXPERTS, d), lambda i: (0, 0)),
                  pl.BlockSpec((N_EXPERTS, 1), lambda i: (0, 0))],
        out_specs=[kt_spec, kt_spec, kt_spec, pl.BlockSpec((N_EXPERTS, 1), lambda i: (0, 0))],
        out_shape=[jax.ShapeDtypeStruct((TOP_K, t), jnp.int32), jax.ShapeDtypeStruct((TOP_K, t), F32),
                   jax.ShapeDtypeStruct((TOP_K, t), jnp.int32), jax.ShapeDtypeStruct((N_EXPERTS, 1), jnp.int32)],
        scratch_shapes=[pltpu.VMEM((N_EXPERTS, 1), F32)],
        compiler_params=_cparams(("arbitrary",)),
        name="router",
    )(x, wt, router_b.astype(F32).reshape(N_EXPERTS, 1))


def _dispatch_kernel(pos_ref, x_ref, xs_hbm, sem):
    tm = x_ref.shape[0]
    for t in range(tm):
        for k in range(TOP_K):
            pltpu.make_async_copy(x_ref.at[pl.ds(t, 1), :], xs_hbm.at[pl.ds(pos_ref[k, t], 1), :],
                                  sem).start(priority=k % 2)
    for k in range(TOP_K):
        pltpu.make_async_copy(x_ref, xs_hbm.at[pl.ds(0, tm), :], sem).wait()


def _dispatch(x, pos, rows, *, tm):
    t, d = x.shape
    tm = _tile(t, tm)
    return pl.pallas_call(
        _dispatch_kernel,
        grid=(t // tm,),
        in_specs=[pl.BlockSpec((TOP_K, tm), lambda i: (0, i), memory_space=pltpu.SMEM),
                  pl.BlockSpec((tm, d), lambda i: (i, 0))],
        out_specs=pl.BlockSpec(memory_space=pl.ANY),
        out_shape=jax.ShapeDtypeStruct((rows, d), x.dtype),
        scratch_shapes=[pltpu.SemaphoreType.DMA(())],
        compiler_params=_cparams(("arbitrary",)),
        name="dispatch",
    )(pos, x)


def _expert_weight_copies(w_hbm, stage, sems, layer, e):
    return [pltpu.make_async_copy(w.at[layer, e], st, sems.at[i]) for i, (w, st) in enumerate(zip(w_hbm, stage))]


def _gmm_kernel(blk_e_ref, nxt_e_ref, nused_ref, xs_ref, wg_hbm, wu_hbm, wd_hbm, o_ref,
                stage_g, stage_u, stage_d, wgu_s, wd_s, sems, *, layer, bm):
    step = pl.program_id(0)
    ff = stage_g.shape[1]
    copies = functools.partial(_expert_weight_copies, (wg_hbm, wu_hbm, wd_hbm), (stage_g, stage_u, stage_d), sems,
                               layer)

    @pl.when(step == 0)
    def _():
        for c in copies(blk_e_ref[0]):
            c.start()

    for sub in range(xs_ref.shape[0] // bm):
        b = step * (xs_ref.shape[0] // bm) + sub
        rows = slice(sub * bm, (sub + 1) * bm)
        used = b < nused_ref[0]
        new_e = jnp.logical_or(b == 0, blk_e_ref[b] != blk_e_ref[jnp.maximum(b - 1, 0)])

        @pl.when(jnp.logical_and(used, new_e))
        def _(b=b):
            for c in copies(blk_e_ref[b]):
                c.wait()
            for r in range(0, stage_g.shape[0], CAST_ROWS):
                wgu_s[r:r + CAST_ROWS, :ff] = stage_g[r:r + CAST_ROWS, :].astype(BF16)
                wgu_s[r:r + CAST_ROWS, ff:] = stage_u[r:r + CAST_ROWS, :].astype(BF16)
            for r in range(0, stage_d.shape[0], CAST_ROWS):
                wd_s[r:r + CAST_ROWS, :] = stage_d[r:r + CAST_ROWS, :].astype(BF16)

            @pl.when(nxt_e_ref[b] >= 0)
            def _():
                for c in copies(nxt_e_ref[b]):
                    c.start()

        @pl.when(used)
        def _(rows=rows):
            x = _unpack_bf16_pairs(xs_ref[rows, :])
            gu = jnp.dot(x, wgu_s[...], preferred_element_type=F32)
            gate = gu[:, :ff]
            hmid = gate * jax.nn.sigmoid(gate) * gu[:, ff:]
            o_ref[rows, :] = _pack_bf16_pairs(jnp.dot(hmid.astype(BF16), wd_s[...], preferred_element_type=F32))


def _gmm(xs, w_gate, w_up, w_down, layer, blk_e, nxt_e, nused, *, bm):
    rows = xs.shape[0]
    d, ff = w_gate.shape[2], w_gate.shape[3]
    n_blocks = rows // bm
    per_step = GMM_BLOCKS_PER_STEP
    assert n_blocks % per_step == 0

    def blk(s, be, nx, nu):
        return (jnp.minimum(s, (nu[0] - 1) // per_step), 0)

    grid_spec = pltpu.PrefetchScalarGridSpec(
        num_scalar_prefetch=3,
        grid=(n_blocks // per_step,),
        in_specs=[pl.BlockSpec((per_step * bm, d // 2), blk),
                  pl.BlockSpec(memory_space=pl.ANY), pl.BlockSpec(memory_space=pl.ANY),
                  pl.BlockSpec(memory_space=pl.ANY)],
        out_specs=pl.BlockSpec((per_step * bm, d // 2), blk),
        scratch_shapes=[pltpu.VMEM((d, ff), F32), pltpu.VMEM((d, ff), F32), pltpu.VMEM((ff, d), F32),
                        pltpu.VMEM((d, 2 * ff), BF16), pltpu.VMEM((ff, d), BF16),
                        pltpu.SemaphoreType.DMA((3,))],
    )
    return pl.pallas_call(
        functools.partial(_gmm_kernel, layer=layer, bm=bm),
        grid_spec=grid_spec,
        out_shape=jax.ShapeDtypeStruct((rows, d // 2), jnp.int32),
        compiler_params=_cparams(("arbitrary",)),
        name="gmm",
    )(blk_e, nxt_e, nused, xs, w_gate, w_up, w_down)


def _combine_kernel(pos_ref, pos_next_ref, os_hbm, gate_ref, x_ref, xb_ref, wgu_ref, wd_ref, g_ref, b_ref,
                    xo_ref, xbo_ref, buf_a, buf_b, y_ref, sems, *, alpha):
    i = pl.program_id(0)
    n_i = pl.num_programs(0)
    tm = buf_a.shape[1]
    ff = wd_ref.shape[0]
    half = buf_a.shape[2]
    bufs = (buf_a, buf_b)

    def row_copy(p, s, k, t):
        return pltpu.make_async_copy(os_hbm.at[pl.ds(p, 1), :], bufs[s].at[k, pl.ds(t, 1), :], sems.at[s])

    def issue(p_ref, col0, s):
        for t in range(tm):
            for k in range(TOP_K):
                row_copy(p_ref[k, col0 + t], s, k, t).start(priority=k % 2)

    def wait_slot(s):
        for k in range(TOP_K):
            pltpu.make_async_copy(os_hbm.at[pl.ds(0, tm), :], bufs[s].at[k], sems.at[s]).wait()

    def finish(r0, s):
        rows = slice(r0, r0 + tm)
        wait_slot(s)
        lo = jnp.zeros((tm, half), F32)
        hi = jnp.zeros((tm, half), F32)
        for k in range(TOP_K):
            u = bufs[s][k]
            w = gate_ref[rows, k:k + 1]
            lo = lo + lax.bitcast_convert_type(lax.shift_left(u, 16), F32) * w
            hi = hi + lax.bitcast_convert_type(jnp.bitwise_and(u, -65536), F32) * w
        routed = jnp.concatenate([lo, hi], axis=1)
        z = _layer_norm(alpha * x_ref[rows, :] + (routed + y_ref[rows, :]), g_ref[...], b_ref[...])
        xo_ref[rows, :] = z
        xbo_ref[rows, :] = z.astype(BF16)

    @pl.when(i == 0)
    def _():
        def body(t, carry):
            for k in range(TOP_K):
                row_copy(pos_ref[k, t], 0, k, t).start(priority=k % 2)
            return carry
        lax.fori_loop(0, tm, body, 0)

    issue(pos_ref, tm, 1)
    gu = jnp.dot(xb_ref[...], wgu_ref[...], preferred_element_type=F32)
    gate = gu[:, :ff]
    hmid = gate * jax.nn.sigmoid(gate) * gu[:, ff:]
    y_ref[...] = jnp.dot(hmid.astype(BF16), wd_ref[...], preferred_element_type=F32)
    finish(0, 0)
    issue(pos_next_ref, 0, 0)
    finish(tm, 1)

    @pl.when(i == n_i - 1)
    def _():
        wait_slot(0)


def _combine(os_, pos, gate_t, x, xb, sh_wgu, sh_wd, g, b, *, alpha, tm):
    t, d = x.shape
    tm = _tile(t // 2, tm)
    ff = sh_wd.shape[0]
    n_i = t // (2 * tm)
    row_spec = pl.BlockSpec((2 * tm, d), lambda i: (i, 0))
    vec_spec = pl.BlockSpec((1, d), lambda i: (0, 0))
    return pl.pallas_call(
        functools.partial(_combine_kernel, alpha=alpha),
        grid=(n_i,),
        in_specs=[pl.BlockSpec((TOP_K, 2 * tm), lambda i: (0, i), memory_space=pltpu.SMEM),
                  pl.BlockSpec((TOP_K, 2 * tm), lambda i: (0, jnp.minimum(i + 1, n_i - 1)), memory_space=pltpu.SMEM),
                  pl.BlockSpec(memory_space=pl.ANY),
                  pl.BlockSpec((2 * tm, TOP_K), lambda i: (i, 0)),
                  row_spec, row_spec,
                  pl.BlockSpec((d, 2 * ff), lambda i: (0, 0)),
                  pl.BlockSpec((ff, d), lambda i: (0, 0)),
                  vec_spec, vec_spec],
        out_specs=[row_spec, row_spec],
        out_shape=[jax.ShapeDtypeStruct((t, d), F32), jax.ShapeDtypeStruct((t, d), BF16)],
        scratch_shapes=[pltpu.VMEM((TOP_K, tm, d // 2), jnp.int32), pltpu.VMEM((TOP_K, tm, d // 2), jnp.int32),
                        pltpu.VMEM((2 * tm, d), F32), pltpu.SemaphoreType.DMA((2,))],
        compiler_params=_cparams(("arbitrary",)),
        name="combine",
    )(pos, pos, os_, gate_t, x, xb, sh_wgu, sh_wd, g.reshape(1, d).astype(F32), b.reshape(1, d).astype(F32))


def _moe(x, xb, xp, router_w, router_b, w_gate, w_up, w_down, layer, sh_wg, sh_wu, sh_wd, g, b, *, alpha):
    t, d = x.shape
    bm = GMM_BLOCK
    eidx, gate, rank, counts = _router(x, router_w, router_b, tm=512)
    counts = counts.reshape(N_EXPERTS)
    padded = (counts + bm - 1) // bm * bm
    pad_end = jnp.cumsum(padded)
    pad_start = pad_end - padded
    n_blocks = t * TOP_K // bm + N_EXPERTS
    nused = (pad_end[-1] // bm).astype(jnp.int32).reshape(1)
    blk_start = jnp.minimum(jnp.arange(n_blocks, dtype=jnp.int32), nused - 1) * bm
    blk_e = jnp.sum((pad_end[None, :] <= blk_start[:, None]).astype(jnp.int32), axis=1)
    blk_e = jnp.minimum(blk_e, N_EXPERTS - 1).astype(jnp.int32)
    e_ids = jnp.arange(N_EXPERTS, dtype=jnp.int32)
    later_used = jnp.logical_and(e_ids[None, :] > e_ids[:, None], counts[None, :] > 0)
    next_used = jnp.min(jnp.where(later_used, e_ids[None, :], N_EXPERTS), axis=1)
    next_used = jnp.where(next_used == N_EXPERTS, -1, next_used).astype(jnp.int32)
    nxt_e = jnp.sum(jnp.where(blk_e[:, None] == e_ids[None, :], next_used[None, :], 0), axis=1).astype(jnp.int32)
    start_of = jnp.sum(jnp.where(eidx[:, :, None] == e_ids, pad_start, 0), axis=-1)
    pos = (start_of + rank).astype(jnp.int32)
    xs = _dispatch(xp, pos, n_blocks * bm, tm=256)
    os_ = _gmm(xs, w_gate, w_up, w_down, layer, blk_e, nxt_e, nused, bm=bm)
    sh_wgu = jnp.concatenate([sh_wg, sh_wu], axis=1).astype(BF16)
    return _combine(os_, pos, gate.T, x, xb, sh_wgu, sh_wd.astype(BF16), g, b, alpha=alpha, tm=128)


def kernel(x, positions, even_w_in, even_w_out, a_lambda, a_subln_g, b_conv_w, odd_w_in, odd_w_out, c_sinks,
           ln1_g, ln1_b, ln2_g, ln2_b, router_w, router_b, exp_w_gate, exp_w_up, exp_w_down,
           sh_w_gate, sh_w_up, sh_w_down):
    bn, s_len, d = x.shape
    depth = ln1_g.shape[0]
    alpha = (2.0 * depth) ** 0.25
    outs = []
    for bi in range(bn):
        xf = x[bi]
        xb = xf
        tab_a = _rope_tables(positions[bi], A_QK_DIM, A_ROT_DIM)
        tab_c = _rope_tables(positions[bi], C_HEAD_DIM, C_ROT_DIM)
        for layer in range(depth):
            j = layer // 2
            if layer % 2 == 0:
                h = _proj_in(xb, even_w_in, j, tab_a, q_width=A_QK_WIDTH, k_width=A_QK_WIDTH,
                             half=A_ROT_DIM // 2, q_scale=A_QK_DIM ** -0.5 * LOG2E, tm=1024, tn=512)
                lam_init = 0.8 - 0.6 * math.exp(-0.3 * layer)
                lv = a_lambda[j].astype(F32)
                lam = (jnp.exp(jnp.sum(lv[0] * lv[1])) - jnp.exp(jnp.sum(lv[2] * lv[3])) + lam_init).reshape(1)
                y_a = _diff_attn(h, lam, a_subln_g[j], lam_init=lam_init, tq=2048, tk=512)
                y_b = _short_conv(h, b_conv_w[j], tr=1024, tc=512)
                ys = [y_a, y_b]
                w_out = even_w_out[j]
            else:
                h = _proj_in(xb, odd_w_in, j, tab_c, q_width=C_Q_WIDTH, k_width=C_KV_WIDTH,
                             half=C_ROT_DIM // 2, q_scale=C_HEAD_DIM ** -0.5 * LOG2E, tm=1024, tn=512)
                ys = [_win_attn(h, c_sinks[j].astype(F32) * LOG2E, tq=256)]
                w_out = odd_w_out[j]
            xf, xb, xp = _proj_out_ln(ys, w_out.astype(BF16), xf, ln1_g[layer], ln1_b[layer], alpha=alpha, tm=256)
            xf, xb = _moe(xf, xb, xp, router_w[layer], router_b[layer], exp_w_gate, exp_w_up, exp_w_down, layer,
                          sh_w_gate[layer], sh_w_up[layer], sh_w_down[layer],
                          ln2_g[layer], ln2_b[layer], alpha=alpha)
        outs.append(xf)
    return jnp.stack(outs, axis=0)
```
